```python
import math
import jax, jax.numpy as jnp
from jax import lax
import numpy as np

D_MODEL = 1024
BATCH = 8
SEQ = 2048
DEPTH = 4
DEC_BATCH = 32
DEC_SEQ = 8
PAST_LEN = 8192
PAGE_SIZE = 128

N_MIXERS = 2
N_CONV = (DEPTH + 1) // 2
N_ATTN = DEPTH // 2
E_CONV = D_MODEL
CONV_WIDTH = 31
HEAD_DIM = 64
N_HEADS = D_MODEL // HEAD_DIM
E_ATTN = N_HEADS * HEAD_DIM
GROUPS = ((128, 1), (512, 4), (2048, 16))
N_GROUPS = len(GROUPS)
MAX_DIL = 16
Q_BLOCK = 128
ALPHA = (2.0 * DEPTH) ** 0.25
BETA = (8.0 * DEPTH) ** -0.25
LN_EPS = 1e-5
NEG_INF = -1e30

kernel_name = "gated_conformer_conv_dilated_swa_deepnorm_step"


def layer_norm(x, g, b):
    xf = x.astype(jnp.float32)
    mu = jnp.mean(xf, axis=-1, keepdims=True)
    var = jnp.mean(jnp.square(xf - mu), axis=-1, keepdims=True)
    y = (xf - mu) * lax.rsqrt(var + LN_EPS) * g.astype(jnp.float32) + b.astype(jnp.float32)
    return y.astype(x.dtype)


def conv_mixer(x, buf, w_in, dw_w, dw_b, ln_g, ln_b, w_out):
    h = jnp.einsum('btd,de->bte', x, w_in)
    a, gl, z = jnp.split(h, 3, axis=-1)
    u = a * jax.nn.sigmoid(gl)
    u_all = jnp.concatenate([buf.astype(u.dtype), u], axis=1)
    c = lax.conv_general_dilated(u_all, dw_w[:, None, :].astype(u.dtype), window_strides=(1,), padding='VALID',
                                 dimension_numbers=('NWC', 'WIO', 'NWC'), feature_group_count=E_CONV)
    c = layer_norm(c + dw_b, ln_g, ln_b)
    y = jax.nn.silu(c) * jax.nn.silu(z)
    out = jnp.einsum('bte,ed->btd', y, w_out)
    return out, u_all[:, -(CONV_WIDTH - 1):]


def dilated_group_attn(q, k_all, v_all, p0, window, dil, qb):
    B, t_pad, H, hd = q.shape
    nb = t_pad // qb
    n_k = (window + qb) // dil
    n_a = qb // dil
    jj = jnp.arange(n_k)[None, :]
    aa = jnp.arange(n_a)[:, None]
    band = (jj >= aa) & (jj <= aa + window // dil)
    slot = jnp.arange(n_k)[:, None] * dil + jnp.arange(dil)[None, :]
    scale = HEAD_DIM ** -0.5

    def block(b):
        start = b * qb
        qs = lax.dynamic_slice_in_dim(q, start, qb, axis=1).reshape(B, n_a, dil, H, hd)
        ks = lax.dynamic_slice_in_dim(k_all, start, window + qb, axis=1).reshape(B, n_k, dil, H, hd)
        vs = lax.dynamic_slice_in_dim(v_all, start, window + qb, axis=1).reshape(B, n_k, dil, H, hd)
        valid = (p0 - window + start + slot) >= 0
        mask = band[None, :, :] & valid.T[:, None, :]
        s = jnp.einsum('barhe,bjrhe->bhraj', qs, ks).astype(jnp.float32) * scale
        s = jnp.where(mask, s, NEG_INF)
        m = jnp.max(s, axis=-1, keepdims=True)
        p = jnp.exp(s - m)
        l = jnp.sum(p, axis=-1, keepdims=True)
        o = jnp.einsum('bhraj,bjrhe->barhe', (p / l).astype(vs.dtype), vs)
        lse = jnp.transpose((m + jnp.log(l))[..., 0], (0, 3, 2, 1))
        return o.reshape(B, qb, H, hd), lse.reshape(B, qb, H)

    o, lse = lax.map(block, jnp.arange(nb))
    o = jnp.moveaxis(o, 0, 1).reshape(B, t_pad, H, hd)
    lse = jnp.moveaxis(lse, 0, 1).reshape(B, t_pad, H)
    return o, lse


def dilated_attn_mixer(x, kv_bufs, p0, w_in, w_out):
    B, T, _ = x.shape
    h = jnp.einsum('btd,de->bte', x, w_in)
    n_qkv = 3 * N_GROUPS * E_ATTN
    qkv = h[..., :n_qkv].reshape(B, T, N_GROUPS, 3, N_HEADS, HEAD_DIM)
    z = h[..., n_qkv:]
    qb = Q_BLOCK if T >= Q_BLOCK else -(-T // MAX_DIL) * MAX_DIL
    t_pad = -(-T // qb) * qb
    q_pad = ((0, 0), (0, t_pad - T), (0, 0), (0, 0))
    outs, lses, new_kv = [], [], []
    for g, (window, dil) in enumerate(GROUPS):
        q, k, v = qkv[:, :, g, 0], qkv[:, :, g, 1], qkv[:, :, g, 2]
        if kv_bufs is None:
            k_ctx, v_ctx = k, v
        else:
            buf = kv_bufs[g].astype(k.dtype)
            k_ctx = jnp.concatenate([buf[:, :, 0], k], axis=1)
            v_ctx = jnp.concatenate([buf[:, :, 1], v], axis=1)
        keep = min(window, p0 + T)
        new_kv.append(jnp.stack([k_ctx[:, -keep:], v_ctx[:, -keep:]], axis=2))
        n_ctx = k_ctx.shape[1]
        kv_pad = ((0, 0), (window + T - n_ctx, t_pad - T), (0, 0), (0, 0))
        o, lse = dilated_group_attn(jnp.pad(q, q_pad), jnp.pad(k_ctx, kv_pad), jnp.pad(v_ctx, kv_pad),
                                    p0, window, dil, qb)
        outs.append(o[:, :T])
        lses.append(lse[:, :T])
    o = jnp.stack(outs, axis=2)
    wgt = jax.nn.softmax(jnp.stack(lses, axis=2), axis=2)
    o = jnp.einsum('btgh,btghe->bthe', wgt.astype(o.dtype), o).reshape(B, T, E_ATTN)
    y = jnp.einsum('bte,ed->btd', o * jax.nn.silu(z), w_out)
    return y, new_kv


def setup_inputs(seed: int = 0) -> dict:
    key = jax.random.key(seed)
    ks = jax.random.split(key, 20)
    f32 = jnp.float32
    nrm = lambda k, s: jax.random.normal(k, s, f32)
    kv_shape = lambda w: (N_ATTN, DEC_BATCH, min(w, PAST_LEN), 2, N_HEADS, HEAD_DIM)
    return {
        "x_prompt": nrm(ks[0], (BATCH, SEQ, D_MODEL)),
        "x_sample": nrm(ks[1], (DEC_BATCH, DEC_SEQ, D_MODEL)),
        "cache_kv_w128_d1": nrm(ks[2], kv_shape(GROUPS[0][0])),
        "cache_kv_w512_d4": nrm(ks[3], kv_shape(GROUPS[1][0])),
        "cache_kv_w2048_d16": nrm(ks[4], kv_shape(GROUPS[2][0])),
        "state_conv": 0.5 * nrm(ks[5], (N_CONV, DEC_BATCH, CONV_WIDTH - 1, E_CONV)),
        "conv_w_in": nrm(ks[6], (N_CONV, D_MODEL, 3 * E_CONV)) * D_MODEL ** -0.5,
        "conv_dw_w": nrm(ks[7], (N_CONV, CONV_WIDTH, E_CONV)) * CONV_WIDTH ** -0.5,
        "conv_dw_b": 0.02 * nrm(ks[8], (N_CONV, E_CONV)),
        "conv_ln_g": 1.0 + 0.02 * nrm(ks[9], (N_CONV, E_CONV)),
        "conv_ln_b": 0.02 * nrm(ks[10], (N_CONV, E_CONV)),
        "conv_w_out": nrm(ks[11], (N_CONV, E_CONV, D_MODEL)) * (E_CONV ** -0.5 * BETA),
        "attn_w_in": nrm(ks[12], (N_ATTN, D_MODEL, (3 * N_GROUPS + 1) * E_ATTN)) * D_MODEL ** -0.5,
        "attn_w_out": nrm(ks[13], (N_ATTN, E_ATTN, D_MODEL)) * (E_ATTN ** -0.5 * BETA),
        "norm_g": 1.0 + 0.02 * nrm(ks[14], (DEPTH, D_MODEL)),
        "norm_b": 0.02 * nrm(ks[15], (DEPTH, D_MODEL)),
    }


def reference(x_prompt, x_sample, cache_kv_w128_d1, cache_kv_w512_d4, cache_kv_w2048_d16, state_conv,
              conv_w_in, conv_dw_w, conv_dw_b, conv_ln_g, conv_ln_b, conv_w_out,
              attn_w_in, attn_w_out, norm_g, norm_b):
    caches = (cache_kv_w128_d1, cache_kv_w512_d4, cache_kv_w2048_d16)
    yp, ys = x_prompt, x_sample
    kv_p = [[] for _ in GROUPS]
    kv_s = [[] for _ in GROUPS]
    conv_p, conv_s = [], []
    for i in range(DEPTH):
        li = i // N_MIXERS
        if i % N_MIXERS == 0:
            prm = (conv_w_in[li], conv_dw_w[li], conv_dw_b[li], conv_ln_g[li], conv_ln_b[li], conv_w_out[li])
            zero_buf = jnp.zeros((yp.shape[0], CONV_WIDTH - 1, E_CONV), yp.dtype)
            fp, bp = conv_mixer(yp, zero_buf, *prm)
            fs, bs = conv_mixer(ys, state_conv[li], *prm)
            conv_p.append(bp)
            conv_s.append(bs)
        else:
            fp, nkp = dilated_attn_mixer(yp, None, 0, attn_w_in[li], attn_w_out[li])
            fs, nks = dilated_attn_mixer(ys, [c[li] for c in caches], PAST_LEN, attn_w_in[li], attn_w_out[li])
            for g in range(N_GROUPS):
                kv_p[g].append(nkp[g])
                kv_s[g].append(nks[g])
        yp = layer_norm(ALPHA * yp + fp, norm_g[i], norm_b[i])
        ys = layer_norm(ALPHA * ys + fs, norm_g[i], norm_b[i])
    return (yp, ys,
            jnp.stack(kv_p[0]), jnp.stack(kv_p[1]), jnp.stack(kv_p[2]), jnp.stack(conv_p),
            jnp.stack(kv_s[0]), jnp.stack(kv_s[1]), jnp.stack(kv_s[2]), jnp.stack(conv_s))
```

```python
import functools

import jax
import jax.numpy as jnp
import numpy as np
from jax import lax
from jax.experimental import pallas as pl
from jax.experimental.pallas import tpu as pltpu

D_MODEL = 1024
DEPTH = 4
E_CONV = D_MODEL
CONV_WIDTH = 31
HEAD_DIM = 64
N_HEADS = D_MODEL // HEAD_DIM
E_ATTN = N_HEADS * HEAD_DIM
GROUPS = ((128, 1), (512, 4), (2048, 16))
N_GROUPS = len(GROUPS)
ALPHA = (2.0 * DEPTH) ** 0.25
LN_EPS = 1e-5
NEG_INF = -1e30
QK_SCALE = HEAD_DIM ** -0.5

LANES = 128
SUBLANES = 8
HALO = 32
HALO_OFF = HALO - (CONV_WIDTH - 1)
Q_BLK = 128
HEADS_PER_STEP = LANES // HEAD_DIM
VMEM_LIMIT = 56 * 1024 * 1024

BF16 = jnp.bfloat16
F32 = jnp.float32


def _dot(a, b):
    return jnp.dot(a, b, preferred_element_type=F32)


def _dot_nt(a, b):
    return lax.dot_general(a, b, (((1,), (1,)), ((), ())), preferred_element_type=F32)


def _sigmoid(x):
    return 1.0 / (1.0 + jnp.exp(-x))


def _layer_norm(x, g, b):
    mu = jnp.mean(x, axis=-1, keepdims=True)
    xc = x - mu
    var = jnp.mean(xc * xc, axis=-1, keepdims=True)
    return xc * lax.rsqrt(var + LN_EPS) * g + b


def _conv_layer_kernel(x_ref, buf_ref, win_ref, dww_ref, dwb_ref, lng_ref, lnb_ref, wout_ref,
                       ng_ref, nb_ref, o_ref, st_ref, ubuf, zbuf, cbuf, *, tt, nt, rc):
    t = pl.program_id(1)

    @pl.when(t == 0)
    def _():
        ubuf[0:HALO, :] = buf_ref[...]

    x = x_ref[...]
    xb = x.astype(BF16)
    a = _dot(xb, win_ref[:, 0:E_CONV])
    gl = _dot(xb, win_ref[:, E_CONV:2 * E_CONV])
    ubuf[HALO:HALO + tt, :] = a * _sigmoid(gl)
    z = _dot(xb, win_ref[:, 2 * E_CONV:3 * E_CONV])
    zbuf[...] = z * _sigmoid(z)

    def lane_chunk(c, carry):
        lanes = pl.ds(pl.multiple_of(c * LANES, LANES), LANES)
        for r0 in range(0, tt, rc):
            acc = jnp.zeros((rc, LANES), F32)
            for k in range(CONV_WIDTH):
                lo = HALO_OFF + k + r0
                acc = acc + dww_ref[k:k + 1, lanes] * ubuf[lo:lo + rc, lanes]
            cbuf[r0:r0 + rc, lanes] = acc
        return carry

    lax.fori_loop(0, E_CONV // LANES, lane_chunk, 0)

    cn = _layer_norm(cbuf[...] + dwb_ref[...], lng_ref[...], lnb_ref[...])
    y = cn * _sigmoid(cn) * zbuf[...]
    f = _dot(y.astype(BF16), wout_ref[...])
    o_ref[...] = _layer_norm(ALPHA * x + f, ng_ref[...], nb_ref[...])

    @pl.when(t == nt - 1)
    def _():
        st_ref[...] = ubuf[tt:tt + HALO, :]

    if nt > 1:
        ubuf[0:HALO, :] = ubuf[tt:tt + HALO, :]


def _conv_layer(x, buf, w_in, dw_w, dw_b, ln_g, ln_b, w_out, ng, nb):
    nbatch, seq, _ = x.shape
    tt = min(seq, 256)
    assert seq % tt == 0 and tt % SUBLANES == 0
    nt = seq // tt
    rc = min(tt, 32)
    buf_pad = jnp.pad(buf, ((0, 0), (HALO_OFF, 0), (0, 0)))
    row = lambda v: v.reshape(1, -1)
    const = lambda shape: pl.BlockSpec(shape, lambda b, t: (0,) * len(shape))
    out, st = pl.pallas_call(
        functools.partial(_conv_layer_kernel, tt=tt, nt=nt, rc=rc),
        grid=(nbatch, nt),
        in_specs=[
            pl.BlockSpec((None, tt, D_MODEL), lambda b, t: (b, t, 0)),
            pl.BlockSpec((None, HALO, E_CONV), lambda b, t: (b, 0, 0)),
            const((D_MODEL, 3 * E_CONV)),
            const((CONV_WIDTH, E_CONV)),
            const((1, E_CONV)), const((1, E_CONV)), const((1, E_CONV)),
            const((E_CONV, D_MODEL)),
            const((1, D_MODEL)), const((1, D_MODEL)),
        ],
        out_specs=[
            pl.BlockSpec((None, tt, D_MODEL), lambda b, t: (b, t, 0)),
            pl.BlockSpec((None, HALO, E_CONV), lambda b, t: (b, 0, 0)),
        ],
        out_shape=[
            jax.ShapeDtypeStruct((nbatch, seq, D_MODEL), F32),
            jax.ShapeDtypeStruct((nbatch, HALO, E_CONV), F32),
        ],
        scratch_shapes=[
            pltpu.VMEM((HALO + tt, E_CONV), F32),
            pltpu.VMEM((tt, E_CONV), F32),
            pltpu.VMEM((tt, E_CONV), F32),
        ],
        compiler_params=pltpu.CompilerParams(
            dimension_semantics=("arbitrary", "arbitrary"), vmem_limit_bytes=VMEM_LIMIT),
        name="conv_layer",
    )(x, buf_pad, w_in, dw_w, row(dw_b), row(ln_g), row(ln_b), w_out, row(ng), row(nb))
    return out, st[:, HALO_OFF:, :]


def _matmul_kernel(x_ref, w_ref, o_ref):
    o_ref[...] = _dot(x_ref[...].astype(BF16), w_ref[...])


def _matmul(x, w):
    m, k = x.shape
    n = w.shape[1]
    tm = min(m, 512)
    tn = 1024
    assert m % tm == 0 and n % tn == 0
    return pl.pallas_call(
        _matmul_kernel,
        grid=(m // tm, n // tn),
        in_specs=[pl.BlockSpec((tm, k), lambda i, j: (i, 0)),
                  pl.BlockSpec((k, tn), lambda i, j: (0, j))],
        out_specs=pl.BlockSpec((tm, tn), lambda i, j: (i, j)),
        out_shape=jax.ShapeDtypeStruct((m, n), F32),
        compiler_params=pltpu.CompilerParams(
            dimension_semantics=("arbitrary", "arbitrary"), vmem_limit_bytes=VMEM_LIMIT),
        name="attn_in_proj",
    )(x, w)


def _proj_ln_kernel(y_ref, x_ref, w_ref, g_ref, b_ref, o_ref):
    f = _dot(y_ref[...].astype(BF16), w_ref[...])
    o_ref[...] = _layer_norm(ALPHA * x_ref[...] + f, g_ref[...], b_ref[...])


def _proj_ln(y, x, w, g, b):
    m, k = y.shape
    n = w.shape[1]
    tm = min(m, 512)
    assert m % tm == 0
    return pl.pallas_call(
        _proj_ln_kernel,
        grid=(m // tm,),
        in_specs=[pl.BlockSpec((tm, k), lambda i: (i, 0)),
                  pl.BlockSpec((tm, n), lambda i: (i, 0)),
                  pl.BlockSpec((k, n), lambda i: (0, 0)),
                  pl.BlockSpec((1, n), lambda i: (0, 0)),
                  pl.BlockSpec((1, n), lambda i: (0, 0))],
        out_specs=pl.BlockSpec((tm, n), lambda i: (i, 0)),
        out_shape=jax.ShapeDtypeStruct((m, n), F32),
        compiler_params=pltpu.CompilerParams(
            dimension_semantics=("arbitrary",), vmem_limit_bytes=VMEM_LIMIT),
        name="attn_out_proj_ln",
    )(y, x, w, g.reshape(1, -1), b.reshape(1, -1))


def _attn_prompt_kernel(*refs, seq):
    qkv_refs = refs[:3 * N_GROUPS]
    z_ref = refs[3 * N_GROUPS]
    y_ref = refs[3 * N_GROUPS + 1]
    o_scr = refs[3 * N_GROUPS + 2: 3 * N_GROUPS + 2 + N_GROUPS]
    l_scr = refs[3 * N_GROUPS + 2 + N_GROUPS:]

    lane = lax.broadcasted_iota(jnp.int32, (Q_BLK, LANES), 1)
    qi = lax.broadcasted_iota(jnp.int32, (Q_BLK, Q_BLK), 0)
    kj = lax.broadcasted_iota(jnp.int32, (Q_BLK, Q_BLK), 1)
    cur_ok = kj <= qi
    prev_ok = kj >= qi
    head_lanes = [(lane >= h * HEAD_DIM) & (lane < (h + 1) * HEAD_DIM) for h in range(HEADS_PER_STEP)]

    def rows(start, dil):
        if dil == 1:
            return pl.ds(start, Q_BLK)
        return pl.ds(start, Q_BLK, stride=dil)

    def unit(g, dil, start, prev_start):
        q_ref, k_ref, v_ref = qkv_refs[3 * g: 3 * g + 3]
        cur = rows(start, dil)
        q = q_ref[cur, :] * QK_SCALE
        kc = k_ref[cur, :].astype(BF16)
        vc = v_ref[cur, :].astype(BF16)
        if prev_start is not None:
            prev = rows(prev_start, dil)
            kp = k_ref[prev, :].astype(BF16)
            vp = v_ref[prev, :].astype(BF16)
        o_all = jnp.zeros((Q_BLK, LANES), F32)
        lse_all = jnp.zeros((Q_BLK, LANES), F32)
        for h in range(HEADS_PER_STEP):
            qh = jnp.where(head_lanes[h], q, 0.0).astype(BF16)
            s_c = jnp.where(cur_ok, _dot_nt(qh, kc), NEG_INF)
            m = jnp.max(s_c, axis=-1, keepdims=True)
            if prev_start is not None:
                s_p = jnp.where(prev_ok, _dot_nt(qh, kp), NEG_INF)
                m = jnp.maximum(m, jnp.max(s_p, axis=-1, keepdims=True))
            p_c = jnp.exp(s_c - m)
            l = jnp.sum(p_c, axis=-1, keepdims=True)
            o = _dot(p_c.astype(BF16), vc)
            if prev_start is not None:
                p_p = jnp.exp(s_p - m)
                l = l + jnp.sum(p_p, axis=-1, keepdims=True)
                o = o + _dot(p_p.astype(BF16), vp)
            o = o / l
            lse = m + jnp.log(l)
            o_all = jnp.where(head_lanes[h], o, o_all)
            lse_all = jnp.where(head_lanes[h], lse, lse_all)
        o_scr[g][cur, :] = o_all
        l_scr[g][cur, :] = lse_all

    for g, (_, dil) in enumerate(GROUPS):
        n_blk = seq // (dil * Q_BLK)

        def per_class(r, carry, g=g, dil=dil, n_blk=n_blk):
            unit(g, dil, r, None)

            def per_block(blk, c2):
                start = r + blk * (dil * Q_BLK)
                unit(g, dil, start, start - dil * Q_BLK)
                return c2

            if n_blk > 1:
                lax.fori_loop(1, n_blk, per_block, 0)
            return carry

        lax.fori_loop(0, dil, per_class, 0)

    chunk = 256

    def merge(i, carry):
        rs = pl.ds(pl.multiple_of(i * chunk, chunk), chunk)
        ls = [l_scr[g][rs, :] for g in range(N_GROUPS)]
        mx = functools.reduce(jnp.maximum, ls)
        es = [jnp.exp(lv - mx) for lv in ls]
        den = functools.reduce(lambda a, b: a + b, es)
        num = functools.reduce(lambda a, b: a + b, [es[g] * o_scr[g][rs, :] for g in range(N_GROUPS)])
        z = z_ref[rs, :]
        y_ref[rs, :] = (num / den) * (z * _sigmoid(z))
        return carry

    lax.fori_loop(0, seq // chunk, merge, 0)


def _attn_prompt(h):
    nbatch, seq, _ = h.shape
    assert seq % (GROUPS[-1][1] * Q_BLK) == 0
    n_col = E_ATTN // LANES

    def col_spec(col0):
        return pl.BlockSpec((None, seq, LANES), lambda b, hp: (b, 0, col0 + hp))

    in_specs = [col_spec((3 * g + j) * n_col) for g in range(N_GROUPS) for j in range(3)]
    in_specs.append(col_spec(3 * N_GROUPS * n_col))
    return pl.pallas_call(
        functools.partial(_attn_prompt_kernel, seq=seq),
        grid=(nbatch, n_col),
        in_specs=in_specs,
        out_specs=pl.BlockSpec((None, seq, LANES), lambda b, hp: (b, 0, hp)),
        out_shape=jax.ShapeDtypeStruct((nbatch, seq, E_ATTN), F32),
        scratch_shapes=[pltpu.VMEM((seq, LANES), F32) for _ in range(2 * N_GROUPS)],
        compiler_params=pltpu.CompilerParams(
            dimension_semantics=("arbitrary", "arbitrary"), vmem_limit_bytes=VMEM_LIMIT),
        name="attn_prompt",
    )(*([h] * (3 * N_GROUPS + 1)))


def _sample_bias(window, dil, t_new, cached_offsets):
    qi = np.arange(N_HEADS * t_new)[:, None] % t_new
    slot = np.arange(LANES)[None, :]
    key_off = np.concatenate([cached_offsets[None, :], slot], axis=1)
    in_range = np.concatenate([np.ones((1, len(cached_offsets)), bool), slot < t_new], axis=1)
    dist = qi - key_off
    ok = in_range & (dist >= 0) & (dist <= window) & (dist % dil == 0)
    return np.where(ok, 0.0, NEG_INF).astype(np.float32)


def _attn_sample_kernel(h_ref, c0_ref, c1_ref, c2_ref, b0_ref, b1_ref, b2_ref, y_ref, k_scr, v_scr, *, t_new):
    n_rows = N_HEADS * t_new
    assert n_rows == LANES and t_new & (t_new - 1) == 0
    new_pad = LANES
    row = lax.broadcasted_iota(jnp.int32, (n_rows, E_ATTN), 0)
    col = lax.broadcasted_iota(jnp.int32, (n_rows, E_ATTN), 1)
    diag = jnp.right_shift(row, t_new.bit_length() - 1) == jnp.right_shift(col, HEAD_DIM.bit_length() - 1)
    bias_refs = (b0_ref, b1_ref, b2_ref)

    outs, lses = [], []
    for g in range(N_GROUPS):
        base = 3 * g * E_ATTN
        q = h_ref[:, base:base + E_ATTN] * QK_SCALE
        k_new = h_ref[:, base + E_ATTN:base + 2 * E_ATTN]
        v_new = h_ref[:, base + 2 * E_ATTN:base + 3 * E_ATTN]
        if g == 0:
            cache = c0_ref[...]
        elif g == 1:
            cache = c1_ref[...]
        else:
            blk = c2_ref[...]
            cache = blk.reshape(blk.shape[0] * blk.shape[1], blk.shape[2])
        n = cache.shape[0]
        k_scr[0:n, :] = cache[:, 0:E_ATTN].astype(BF16)
        v_scr[0:n, :] = cache[:, E_ATTN:2 * E_ATTN].astype(BF16)
        pad = jnp.zeros((new_pad - t_new, E_ATTN), F32)
        k_scr[n:n + new_pad, :] = jnp.concatenate([k_new, pad], axis=0).astype(BF16)
        v_scr[n:n + new_pad, :] = jnp.concatenate([v_new, pad], axis=0).astype(BF16)
        n_all = n + new_pad

        q_rep = jnp.broadcast_to(q[None], (N_HEADS, t_new, E_ATTN)).reshape(n_rows, E_ATTN)
        q_bd = jnp.where(diag, q_rep, 0.0).astype(BF16)
        s = _dot_nt(q_bd, k_scr[0:n_all, :]) + bias_refs[g][...]
        m = jnp.max(s, axis=-1, keepdims=True)
        p = jnp.exp(s - m)
        l = jnp.sum(p, axis=-1, keepdims=True)
        o = _dot(p.astype(BF16), v_scr[0:n_all, :]) / l
        lse = m + jnp.log(l)
        o = jnp.where(diag, o, 0.0).reshape(N_HEADS, t_new, E_ATTN).sum(axis=0)
        lse = jnp.where(diag, lse, 0.0).reshape(N_HEADS, t_new, E_ATTN).sum(axis=0)
        outs.append(o)
        lses.append(lse)

    mx = functools.reduce(jnp.maximum, lses)
    es = [jnp.exp(lv - mx) for lv in lses]
    den = functools.reduce(lambda a, b: a + b, es)
    num = functools.reduce(lambda a, b: a + b, [e * o for e, o in zip(es, outs)])
    z = h_ref[:, 3 * N_GROUPS * E_ATTN:]
    y_ref[...] = (num / den) * (z * _sigmoid(z))


def _attn_sample(h, caches, li):
    nbatch, t_new, width = h.shape
    w0, w1, w2 = (g[0] for g in GROUPS)
    dil2 = GROUPS[2][1]
    assert t_new <= dil2 and w2 % dil2 == 0
    c0 = caches[0].reshape(caches[0].shape[0], nbatch, w0, 2 * E_ATTN)
    c1 = caches[1].reshape(caches[1].shape[0], nbatch, w1, 2 * E_ATTN)
    c2 = caches[2].reshape(caches[2].shape[0], nbatch, w2 // dil2, dil2, 2 * E_ATTN)
    n2 = (w2 // dil2) * t_new
    off0 = np.arange(w0) - w0
    off1 = np.arange(w1) - w1
    off2 = (np.arange(n2) // t_new) * dil2 + np.arange(n2) % t_new - w2
    biases = [jnp.asarray(_sample_bias(GROUPS[g][0], GROUPS[g][1], t_new, off))
              for g, off in enumerate((off0, off1, off2))]
    n_max = max(w0, w1, n2) + LANES
    full = lambda a: pl.BlockSpec(a.shape, lambda b: (0, 0))
    return pl.pallas_call(
        functools.partial(_attn_sample_kernel, t_new=t_new),
        grid=(nbatch,),
        in_specs=[
            pl.BlockSpec((None, t_new, width), lambda b: (b, 0, 0)),
            pl.BlockSpec((None, None, w0, 2 * E_ATTN), lambda b: (li, b, 0, 0)),
            pl.BlockSpec((None, None, w1, 2 * E_ATTN), lambda b: (li, b, 0, 0)),
            pl.BlockSpec((None, None, w2 // dil2, t_new, 2 * E_ATTN), lambda b: (li, b, 0, 0, 0)),
            full(biases[0]), full(biases[1]), full(biases[2]),
        ],
        out_specs=pl.BlockSpec((None, t_new, E_ATTN), lambda b: (b, 0, 0)),
        out_shape=jax.ShapeDtypeStruct((nbatch, t_new, E_ATTN), F32),
        scratch_shapes=[pltpu.VMEM((n_max, E_ATTN), BF16), pltpu.VMEM((n_max, E_ATTN), BF16)],
        compiler_params=pltpu.CompilerParams(
            dimension_semantics=("arbitrary",), vmem_limit_bytes=VMEM_LIMIT),
        name="attn_sample",
    )(h, c0, c1, c2, *biases)


def kernel(x_prompt, x_sample, cache_kv_w128_d1, cache_kv_w512_d4, cache_kv_w2048_d16, state_conv,
           conv_w_in, conv_dw_w, conv_dw_b, conv_ln_g, conv_ln_b, conv_w_out,
           attn_w_in, attn_w_out, norm_g, norm_b):
    caches = (cache_kv_w128_d1, cache_kv_w512_d4, cache_kv_w2048_d16)
    conv_w_in_b = conv_w_in.astype(BF16)
    conv_w_out_b = conv_w_out.astype(BF16)
    attn_w_in_b = attn_w_in.astype(BF16)
    attn_w_out_b = attn_w_out.astype(BF16)

    yp, ys = x_prompt, x_sample
    bp, tp, _ = yp.shape
    bs, ts, _ = ys.shape
    kv_p = [[] for _ in GROUPS]
    kv_s = [[] for _ in GROUPS]
    conv_p, conv_s = [], []
    for i in range(DEPTH):
        li = i // 2
        if i % 2 == 0:
            prm = (conv_w_in_b[li], conv_dw_w[li], conv_dw_b[li], conv_ln_g[li], conv_ln_b[li],
                   conv_w_out_b[li], norm_g[i], norm_b[i])
            zero_buf = jnp.zeros((bp, CONV_WIDTH - 1, E_CONV), F32)
            yp, stp = _conv_layer(yp, zero_buf, *prm)
            ys, sts = _conv_layer(ys, state_conv[li], *prm)
            conv_p.append(stp)
            conv_s.append(sts)
        else:
            hp = _matmul(yp.reshape(bp * tp, D_MODEL), attn_w_in_b[li]).reshape(bp, tp, -1)
            hs = _matmul(ys.reshape(bs * ts, D_MODEL), attn_w_in_b[li]).reshape(bs, ts, -1)
            op = _attn_prompt(hp)
            os_ = _attn_sample(hs, caches, li)
            yp = _proj_ln(op.reshape(bp * tp, E_ATTN), yp.reshape(bp * tp, D_MODEL),
                          attn_w_out_b[li], norm_g[i], norm_b[i]).reshape(bp, tp, D_MODEL)
            ys = _proj_ln(os_.reshape(bs * ts, E_ATTN), ys.reshape(bs * ts, D_MODEL),
                          attn_w_out_b[li], norm_g[i], norm_b[i]).reshape(bs, ts, D_MODEL)
            for g, (window, _) in enumerate(GROUPS):
                c0 = (3 * g + 1) * E_ATTN
                keep = min(window, tp)
                kv_p[g].append(hp[:, tp - keep:, c0:c0 + 2 * E_ATTN].reshape(bp, keep, 2, N_HEADS, HEAD_DIM))
                new = hs[:, :, c0:c0 + 2 * E_ATTN].reshape(bs, ts, 2, N_HEADS, HEAD_DIM)
                kv_s[g].append(jnp.concatenate([caches[g][li][:, ts:], new], axis=1))
    return (yp, ys,
            jnp.stack(kv_p[0]), jnp.stack(kv_p[1]), jnp.stack(kv_p[2]), jnp.stack(conv_p),
            jnp.stack(kv_s[0]), jnp.stack(kv_s[1]), jnp.stack(kv_s[2]), jnp.stack(conv_s))
```

```python
import functools

import jax
import jax.numpy as jnp
import numpy as np
from jax import lax
from jax.experimental import pallas as pl
from jax.experimental.pallas import tpu as pltpu

D_MODEL = 1024
DEPTH = 4
E_CONV = D_MODEL
CONV_WIDTH = 31
HEAD_DIM = 64
N_HEADS = D_MODEL // HEAD_DIM
E_ATTN = N_HEADS * HEAD_DIM
GROUPS = ((128, 1), (512, 4), (2048, 16))
N_GROUPS = len(GROUPS)
ALPHA = (2.0 * DEPTH) ** 0.25
LN_EPS = 1e-5
NEG_INF = -1e30
QK_SCALE = HEAD_DIM ** -0.5

LANES = 128
SUBLANES = 8
HALO = 32
HALO_OFF = HALO - (CONV_WIDTH - 1)
Q_BLK = 128
HEADS_PER_STEP = LANES // HEAD_DIM
SAMPLE_HEADS = 4
VMEM_LIMIT = 56 * 1024 * 1024

BF16 = jnp.bfloat16
F32 = jnp.float32


def _dot(a, b):
    return jnp.dot(a, b, preferred_element_type=F32)


def _dot_nt(a, b):
    return lax.dot_general(a, b, (((1,), (1,)), ((), ())), preferred_element_type=F32)


def _sigmoid(x):
    return 1.0 / (1.0 + jnp.exp(-x))


def _layer_norm(x, g, b):
    mu = jnp.mean(x, axis=-1, keepdims=True)
    xc = x - mu
    var = jnp.mean(xc * xc, axis=-1, keepdims=True)
    return xc * lax.rsqrt(var + LN_EPS) * g + b


def _merge_groups(outs, lses):
    mx = functools.reduce(jnp.maximum, lses)
    es = [jnp.exp(lv - mx) for lv in lses]
    den = functools.reduce(lambda a, b: a + b, es)
    num = functools.reduce(lambda a, b: a + b, [e * o for e, o in zip(es, outs)])
    return num / den


def _conv_layer_kernel(x_ref, buf_ref, win_ref, dww_ref, dwb_ref, lng_ref, lnb_ref, wout_ref,
                       ng_ref, nb_ref, o_ref, st_ref, ubuf, zbuf, cbuf, *, tt, nt, rc):
    t = pl.program_id(1)

    @pl.when(t == 0)
    def _():
        ubuf[0:HALO, :] = buf_ref[...]

    x = x_ref[...]
    xb = x.astype(BF16)
    a = _dot(xb, win_ref[:, 0:E_CONV])
    gl = _dot(xb, win_ref[:, E_CONV:2 * E_CONV])
    ubuf[HALO:HALO + tt, :] = a * _sigmoid(gl)
    z = _dot(xb, win_ref[:, 2 * E_CONV:3 * E_CONV])
    zbuf[...] = z * _sigmoid(z)

    def lane_chunk(c, carry):
        lanes = pl.ds(pl.multiple_of(c * LANES, LANES), LANES)
        for r0 in range(0, tt, rc):
            acc = jnp.zeros((rc, LANES), F32)
            for k in range(CONV_WIDTH):
                lo = HALO_OFF + k + r0
                acc = acc + dww_ref[k:k + 1, lanes] * ubuf[lo:lo + rc, lanes]
            cbuf[r0:r0 + rc, lanes] = acc
        return carry

    lax.fori_loop(0, E_CONV // LANES, lane_chunk, 0)

    cn = _layer_norm(cbuf[...] + dwb_ref[...], lng_ref[...], lnb_ref[...])
    y = cn * _sigmoid(cn) * zbuf[...]
    f = _dot(y.astype(BF16), wout_ref[...])
    o_ref[...] = _layer_norm(ALPHA * x + f, ng_ref[...], nb_ref[...])

    @pl.when(t == nt - 1)
    def _():
        st_ref[...] = ubuf[tt:tt + HALO, :]

    if nt > 1:
        ubuf[0:HALO, :] = ubuf[tt:tt + HALO, :]


def _conv_layer(x, buf, w_in, dw_w, dw_b, ln_g, ln_b, w_out, ng, nb):
    nbatch, seq, _ = x.shape
    tt = min(seq, 256)
    assert seq % tt == 0 and tt % SUBLANES == 0
    nt = seq // tt
    rc = min(tt, 32)
    buf_pad = jnp.pad(buf, ((0, 0), (HALO_OFF, 0), (0, 0)))
    row = lambda v: v.reshape(1, -1)
    const = lambda shape: pl.BlockSpec(shape, lambda b, t: (0,) * len(shape))
    out, st = pl.pallas_call(
        functools.partial(_conv_layer_kernel, tt=tt, nt=nt, rc=rc),
        grid=(nbatch, nt),
        in_specs=[
            pl.BlockSpec((None, tt, D_MODEL), lambda b, t: (b, t, 0)),
            pl.BlockSpec((None, HALO, E_CONV), lambda b, t: (b, 0, 0)),
            const((D_MODEL, 3 * E_CONV)),
            const((CONV_WIDTH, E_CONV)),
            const((1, E_CONV)), const((1, E_CONV)), const((1, E_CONV)),
            const((E_CONV, D_MODEL)),
            const((1, D_MODEL)), const((1, D_MODEL)),
        ],
        out_specs=[
            pl.BlockSpec((None, tt, D_MODEL), lambda b, t: (b, t, 0)),
            pl.BlockSpec((None, HALO, E_CONV), lambda b, t: (b, 0, 0)),
        ],
        out_shape=[
            jax.ShapeDtypeStruct((nbatch, seq, D_MODEL), F32),
            jax.ShapeDtypeStruct((nbatch, HALO, E_CONV), F32),
        ],
        scratch_shapes=[
            pltpu.VMEM((HALO + tt, E_CONV), F32),
            pltpu.VMEM((tt, E_CONV), F32),
            pltpu.VMEM((tt, E_CONV), F32),
        ],
        compiler_params=pltpu.CompilerParams(
            dimension_semantics=("arbitrary", "arbitrary"), vmem_limit_bytes=VMEM_LIMIT),
        name="conv_layer",
    )(x, buf_pad, w_in, dw_w, row(dw_b), row(ln_g), row(ln_b), w_out, row(ng), row(nb))
    return out, st[:, HALO_OFF:, :]


def _matmul_kernel(x_ref, w_ref, o_ref):
    o_ref[...] = _dot(x_ref[...].astype(BF16), w_ref[...])


def _matmul(x, w):
    m, k = x.shape
    n = w.shape[1]
    tm = min(m, 512)
    tn = 1024
    assert m % tm == 0 and n % tn == 0
    return pl.pallas_call(
        _matmul_kernel,
        grid=(m // tm, n // tn),
        in_specs=[pl.BlockSpec((tm, k), lambda i, j: (i, 0)),
                  pl.BlockSpec((k, tn), lambda i, j: (0, j))],
        out_specs=pl.BlockSpec((tm, tn), lambda i, j: (i, j)),
        out_shape=jax.ShapeDtypeStruct((m, n), F32),
        compiler_params=pltpu.CompilerParams(
            dimension_semantics=("arbitrary", "arbitrary"), vmem_limit_bytes=VMEM_LIMIT),
        name="attn_in_proj",
    )(x, w)


def _proj_ln_kernel(y_ref, x_ref, w_ref, g_ref, b_ref, o_ref):
    f = _dot(y_ref[...].astype(BF16), w_ref[...])
    o_ref[...] = _layer_norm(ALPHA * x_ref[...] + f, g_ref[...], b_ref[...])


def _proj_ln(y, x, w, g, b):
    m, k = y.shape
    n = w.shape[1]
    tm = min(m, 512)
    assert m % tm == 0
    return pl.pallas_call(
        _proj_ln_kernel,
        grid=(m // tm,),
        in_specs=[pl.BlockSpec((tm, k), lambda i: (i, 0)),
                  pl.BlockSpec((tm, n), lambda i: (i, 0)),
                  pl.BlockSpec((k, n), lambda i: (0, 0)),
                  pl.BlockSpec((1, n), lambda i: (0, 0)),
                  pl.BlockSpec((1, n), lambda i: (0, 0))],
        out_specs=pl.BlockSpec((tm, n), lambda i: (i, 0)),
        out_shape=jax.ShapeDtypeStruct((m, n), F32),
        compiler_params=pltpu.CompilerParams(
            dimension_semantics=("arbitrary",), vmem_limit_bytes=VMEM_LIMIT),
        name="attn_out_proj_ln",
    )(y, x, w, g.reshape(1, -1), b.reshape(1, -1))


def _attn_prompt_kernel(*refs, seq):
    qkv_refs = refs[:3 * N_GROUPS]
    z_ref = refs[3 * N_GROUPS]
    y_ref = refs[3 * N_GROUPS + 1]
    o_scr = refs[3 * N_GROUPS + 2: 3 * N_GROUPS + 2 + N_GROUPS]
    l_scr = refs[3 * N_GROUPS + 2 + N_GROUPS:]

    n_stack = HEADS_PER_STEP * Q_BLK
    lane = lax.broadcasted_iota(jnp.int32, (Q_BLK, LANES), 1)
    first_head = lane < HEAD_DIM
    qi = lax.broadcasted_iota(jnp.int32, (n_stack, Q_BLK), 0) & (Q_BLK - 1)
    kj = lax.broadcasted_iota(jnp.int32, (n_stack, Q_BLK), 1)
    bias_cur = jnp.where(kj <= qi, 0.0, NEG_INF)
    bias_prev = jnp.where(kj >= qi, 0.0, NEG_INF)
    bias_both = jnp.concatenate([bias_prev, bias_cur], axis=1)

    def rows(start, dil):
        if dil == 1:
            return pl.ds(start, Q_BLK)
        return pl.ds(start, Q_BLK, stride=dil)

    def unit_batch(g, dil, starts, has_prev, lead_start=None):
        q_ref, k_ref, v_ref = qkv_refs[3 * g: 3 * g + 3]
        ks = [k_ref[rows(st, dil), :].astype(BF16) for st in starts]
        vs = [v_ref[rows(st, dil), :].astype(BF16) for st in starts]
        if lead_start is not None:
            k_lead = k_ref[rows(lead_start, dil), :].astype(BF16)
            v_lead = v_ref[rows(lead_start, dil), :].astype(BF16)
        for a, st in enumerate(starts):
            q = q_ref[rows(st, dil), :] * QK_SCALE
            qs = jnp.concatenate([jnp.where(first_head, q, 0.0), jnp.where(first_head, 0.0, q)],
                                 axis=0).astype(BF16)
            if has_prev[a]:
                kp, vp = (ks[a - 1], vs[a - 1]) if a > 0 else (k_lead, v_lead)
                kk = jnp.concatenate([kp, ks[a]], axis=0)
                vv = jnp.concatenate([vp, vs[a]], axis=0)
                bias = bias_both
            else:
                kk, vv, bias = ks[a], vs[a], bias_cur
            s = _dot_nt(qs, kk) + bias
            m = jnp.max(s, axis=-1, keepdims=True)
            p = jnp.exp(s - m)
            l = jnp.sum(p, axis=-1, keepdims=True)
            o = _dot(p.astype(BF16), vv) * (1.0 / l)
            lse = m + jnp.log(l)
            o_scr[g][rows(st, dil), :] = jnp.where(first_head, o[0:Q_BLK], o[Q_BLK:])
            l_scr[g][rows(st, dil), :] = jnp.where(first_head, lse[0:Q_BLK], lse[Q_BLK:])

    per_iter = 4
    for g, (_, dil) in enumerate(GROUPS):
        n_blk = seq // (dil * Q_BLK)
        step = dil * Q_BLK
        if n_blk == 1:
            def classes(i, carry, g=g, dil=dil):
                r0 = i * per_iter
                unit_batch(g, dil, [r0 + a for a in range(per_iter)], [False] * per_iter)
                return carry

            lax.fori_loop(0, dil // per_iter, classes, 0)
        else:
            assert n_blk % per_iter == 0

            def per_class(r, carry, g=g, dil=dil, n_blk=n_blk, step=step):
                unit_batch(g, dil, [r + a * step for a in range(per_iter)],
                           [False] + [True] * (per_iter - 1))

                def later(i, c2):
                    s0 = r + i * (per_iter * step)
                    unit_batch(g, dil, [s0 + a * step for a in range(per_iter)], [True] * per_iter,
                               lead_start=s0 - step)
                    return c2

                if n_blk > per_iter:
                    lax.fori_loop(1, n_blk // per_iter, later, 0)
                return carry

            lax.fori_loop(0, dil, per_class, 0)

    chunk = 256

    def merge(i, carry):
        rs = pl.ds(pl.multiple_of(i * chunk, chunk), chunk)
        o = _merge_groups([o_scr[g][rs, :] for g in range(N_GROUPS)],
                          [l_scr[g][rs, :] for g in range(N_GROUPS)])
        z = z_ref[rs, :]
        y_ref[rs, :] = o * (z * _sigmoid(z))
        return carry

    lax.fori_loop(0, seq // chunk, merge, 0)


def _attn_prompt(h):
    nbatch, seq, _ = h.shape
    assert seq % (GROUPS[-1][1] * Q_BLK) == 0
    n_col = E_ATTN // LANES

    def col_spec(col0):
        return pl.BlockSpec((None, seq, LANES), lambda b, hp: (b, 0, col0 + hp))

    in_specs = [col_spec((3 * g + j) * n_col) for g in range(N_GROUPS) for j in range(3)]
    in_specs.append(col_spec(3 * N_GROUPS * n_col))
    return pl.pallas_call(
        functools.partial(_attn_prompt_kernel, seq=seq),
        grid=(nbatch, n_col),
        in_specs=in_specs,
        out_specs=pl.BlockSpec((None, seq, LANES), lambda b, hp: (b, 0, hp)),
        out_shape=jax.ShapeDtypeStruct((nbatch, seq, E_ATTN), F32),
        scratch_shapes=[pltpu.VMEM((seq, LANES), F32) for _ in range(2 * N_GROUPS)],
        compiler_params=pltpu.CompilerParams(
            dimension_semantics=("arbitrary", "arbitrary"), vmem_limit_bytes=VMEM_LIMIT),
        name="attn_prompt",
    )(*([h] * (3 * N_GROUPS + 1)))


def _kv_transpose_kernel(*refs):
    x_ref, o_ref = refs[0], refs[-1]
    o_ref[...] = x_ref[...].T


def _kv_transpose(h, g, li, n_layers, prev):
    nbatch, seq, _ = h.shape
    keep = min(GROUPS[g][0], seq)
    tt = min(keep, 512)
    assert keep % tt == 0 and (seq - keep) % tt == 0
    row0 = (seq - keep) // tt
    col0 = 3 * g + 1
    in_specs = [pl.BlockSpec((None, tt, E_ATTN), lambda b, kv, j: (b, row0 + j, col0 + kv))]
    args = [h]
    aliases = {}
    if prev is not None:
        in_specs.append(pl.BlockSpec(memory_space=pl.ANY))
        args.append(prev)
        aliases = {1: 0}
    return pl.pallas_call(
        _kv_transpose_kernel,
        grid=(nbatch, 2, keep // tt),
        in_specs=in_specs,
        out_specs=pl.BlockSpec((None, None, E_ATTN, tt), lambda b, kv, j: (li, b, kv, j)),
        out_shape=jax.ShapeDtypeStruct((n_layers, nbatch, 2 * E_ATTN, keep), F32),
        input_output_aliases=aliases,
        compiler_params=pltpu.CompilerParams(
            dimension_semantics=("arbitrary", "arbitrary", "arbitrary"), vmem_limit_bytes=VMEM_LIMIT),
        name="kv_transpose",
    )(*args)


def _sample_biases(window, dil, t_new, n_heads):
    qi = np.arange(n_heads * t_new)[:, None] % t_new
    old_off = np.arange(window)[None, :] - window
    new_off = np.arange(LANES)[None, :] - (LANES - t_new)
    res = []
    for off, slot_ok in ((old_off, old_off < 0), (new_off, new_off >= 0)):
        dist = qi - off
        ok = slot_ok & (dist >= 0) & (dist <= window) & (dist % dil == 0)
        res.append(np.where(ok, 0.0, NEG_INF).astype(np.float32))
    return res


def _attn_sample_kernel(*refs, t_new):
    n_in = 10 + 3 * N_GROUPS
    h_refs = refs[:10]
    c_refs = refs[10:10 + N_GROUPS]
    b_refs = refs[10 + N_GROUPS:n_in]
    y_ref = refs[-(N_GROUPS + 1)]
    o_refs = refs[-N_GROUPS:]

    width = SAMPLE_HEADS * HEAD_DIM
    n_rows = SAMPLE_HEADS * t_new
    row = lax.broadcasted_iota(jnp.int32, (n_rows, width), 0)
    col = lax.broadcasted_iota(jnp.int32, (n_rows, width), 1)
    diag = jnp.right_shift(row, t_new.bit_length() - 1) == jnp.right_shift(col, HEAD_DIM.bit_length() - 1)
    new_lanes = lax.broadcasted_iota(jnp.int32, (width, LANES), 1) >= LANES - t_new
    zpad = jnp.zeros((LANES - t_new, width), F32)

    outs, lses = [], []
    for g, (window, _) in enumerate(GROUPS):
        q = h_refs[3 * g][...] * QK_SCALE
        kt_new = jnp.concatenate([zpad, h_refs[3 * g + 1][...]], axis=0).T
        vt_new = jnp.concatenate([zpad, h_refs[3 * g + 2][...]], axis=0).T
        kt = c_refs[g][0]
        vt = c_refs[g][1]
        for idx, (old, new) in enumerate(((kt, kt_new), (vt, vt_new))):
            rolled = pltpu.roll(old, window - t_new, axis=1)
            if window > LANES:
                o_refs[g][idx, :, 0:window - LANES] = rolled[:, 0:window - LANES]
            o_refs[g][idx, :, window - LANES:window] = jnp.where(new_lanes, new, rolled[:, window - LANES:])

        q_rep = jnp.broadcast_to(q[None], (SAMPLE_HEADS, t_new, width)).reshape(n_rows, width)
        q_bd = jnp.where(diag, q_rep, 0.0).astype(BF16)
        s_old = _dot(q_bd, kt.astype(BF16)) + b_refs[2 * g][...]
        s_new = _dot(q_bd, kt_new.astype(BF16)) + b_refs[2 * g + 1][...]
        m = jnp.maximum(jnp.max(s_old, axis=-1, keepdims=True), jnp.max(s_new, axis=-1, keepdims=True))
        p_old = jnp.exp(s_old - m)
        p_new = jnp.exp(s_new - m)
        l = jnp.sum(p_old, axis=-1, keepdims=True) + jnp.sum(p_new, axis=-1, keepdims=True)
        o = (_dot_nt(p_old.astype(BF16), vt.astype(BF16))
             + _dot_nt(p_new.astype(BF16), vt_new.astype(BF16))) * (1.0 / l)
        lse = m + jnp.log(l)
        outs.append(jnp.where(diag, o, 0.0).reshape(SAMPLE_HEADS, t_new, width).sum(axis=0))
        lses.append(jnp.where(diag, lse, 0.0).reshape(SAMPLE_HEADS, t_new, width).sum(axis=0))

    z = h_refs[9][...]
    y_ref[...] = _merge_groups(outs, lses) * (z * _sigmoid(z))


def _attn_sample(h, caches_t, li, prevs):
    nbatch, t_new, _ = h.shape
    n_layers = caches_t[0].shape[0]
    width = SAMPLE_HEADS * HEAD_DIM
    n_chunk = E_ATTN // width
    assert t_new & (t_new - 1) == 0 and t_new <= SUBLANES

    def h_spec(blk):
        return pl.BlockSpec((None, t_new, width), lambda b, c: (b, 0, blk * n_chunk + c))

    def cache_spec(window):
        return pl.BlockSpec((None, None, 2, width, window), lambda b, c: (li, b, 0, c, 0))

    biases = []
    for window, dil in GROUPS:
        biases += [jnp.asarray(a) for a in _sample_biases(window, dil, t_new, SAMPLE_HEADS)]
    full = lambda a: pl.BlockSpec(a.shape, lambda b, c: (0, 0))

    in_specs = [h_spec(blk) for blk in range(10)]
    in_specs += [cache_spec(window) for window, _ in GROUPS]
    in_specs += [full(a) for a in biases]
    args = [h] * 10 + list(caches_t) + biases
    aliases = {}
    if prevs is not None:
        for g in range(N_GROUPS):
            aliases[len(args)] = 1 + g
            in_specs.append(pl.BlockSpec(memory_space=pl.ANY))
            args.append(prevs[g])
    res = pl.pallas_call(
        functools.partial(_attn_sample_kernel, t_new=t_new),
        grid=(nbatch, n_chunk),
        in_specs=in_specs,
        out_specs=[pl.BlockSpec((None, t_new, width), lambda b, c: (b, 0, c))]
        + [cache_spec(window) for window, _ in GROUPS],
        out_shape=[jax.ShapeDtypeStruct((nbatch, t_new, E_ATTN), F32)]
        + [jax.ShapeDtypeStruct((n_layers, nbatch, 2, E_ATTN, window), F32) for window, _ in GROUPS],
        input_output_aliases=aliases,
        compiler_params=pltpu.CompilerParams(
            dimension_semantics=("arbitrary", "arbitrary"), vmem_limit_bytes=VMEM_LIMIT),
        name="attn_sample",
    )(*args)
    return res[0], list(res[1:])


def _cache_time_minor(c):
    n_layers, nbatch, window = c.shape[:3]
    return jnp.transpose(c, (0, 1, 3, 4, 5, 2)).reshape(n_layers, nbatch, 2, E_ATTN, window)


def _cache_time_major(c):
    n_layers, nbatch, window = c.shape[0], c.shape[1], c.shape[-1]
    c = c.reshape(n_layers, nbatch, 2, N_HEADS, HEAD_DIM, window)
    return jnp.transpose(c, (0, 1, 5, 2, 3, 4))


def kernel(x_prompt, x_sample, cache_kv_w128_d1, cache_kv_w512_d4, cache_kv_w2048_d16, state_conv,
           conv_w_in, conv_dw_w, conv_dw_b, conv_ln_g, conv_ln_b, conv_w_out,
           attn_w_in, attn_w_out, norm_g, norm_b):
    caches_t = [_cache_time_minor(c) for c in (cache_kv_w128_d1, cache_kv_w512_d4, cache_kv_w2048_d16)]
    n_attn = attn_w_in.shape[0]
    conv_w_in_b = conv_w_in.astype(BF16)
    conv_w_out_b = conv_w_out.astype(BF16)
    attn_w_in_b = attn_w_in.astype(BF16)
    attn_w_out_b = attn_w_out.astype(BF16)

    yp, ys = x_prompt, x_sample
    bp, tp, _ = yp.shape
    bs, ts, _ = ys.shape
    kv_p = [None] * N_GROUPS
    kv_s = None
    conv_p, conv_s = [], []
    for i in range(DEPTH):
        li = i // 2
        if i % 2 == 0:
            prm = (conv_w_in_b[li], conv_dw_w[li], conv_dw_b[li], conv_ln_g[li], conv_ln_b[li],
                   conv_w_out_b[li], norm_g[i], norm_b[i])
            zero_buf = jnp.zeros((bp, CONV_WIDTH - 1, E_CONV), F32)
            yp, stp = _conv_layer(yp, zero_buf, *prm)
            ys, sts = _conv_layer(ys, state_conv[li], *prm)
            conv_p.append(stp)
            conv_s.append(sts)
        else:
            hp = _matmul(yp.reshape(bp * tp, D_MODEL), attn_w_in_b[li]).reshape(bp, tp, -1)
            hs = _matmul(ys.reshape(bs * ts, D_MODEL), attn_w_in_b[li]).reshape(bs, ts, -1)
            op = _attn_prompt(hp)
            os_, kv_s = _attn_sample(hs, caches_t, li, kv_s)
            for g in range(N_GROUPS):
                kv_p[g] = _kv_transpose(hp, g, li, n_attn, kv_p[g])
            yp = _proj_ln(op.reshape(bp * tp, E_ATTN), yp.reshape(bp * tp, D_MODEL),
                          attn_w_out_b[li], norm_g[i], norm_b[i]).reshape(bp, tp, D_MODEL)
            ys = _proj_ln(os_.reshape(bs * ts, E_ATTN), ys.reshape(bs * ts, D_MODEL),
                          attn_w_out_b[li], norm_g[i], norm_b[i]).reshape(bs, ts, D_MODEL)
    return (yp, ys,
            _cache_time_major(kv_p[0]), _cache_time_major(kv_p[1]), _cache_time_major(kv_p[2]),
            jnp.stack(conv_p),
            _cache_time_major(kv_s[0]), _cache_time_major(kv_s[1]), _cache_time_major(kv_s[2]),
            jnp.stack(conv_s))
```

```python
import functools

import jax
import jax.numpy as jnp
import numpy as np
from jax import lax
from jax.experimental import pallas as pl
from jax.experimental.pallas import tpu as pltpu

D_MODEL = 1024
DEPTH = 4
E_CONV = D_MODEL
CONV_WIDTH = 31
HEAD_DIM = 64
N_HEADS = D_MODEL // HEAD_DIM
E_ATTN = N_HEADS * HEAD_DIM
GROUPS = ((128, 1), (512, 4), (2048, 16))
N_GROUPS = len(GROUPS)
ALPHA = (2.0 * DEPTH) ** 0.25
LN_EPS = 1e-5
NEG_INF = -1e30
QK_SCALE = HEAD_DIM ** -0.5

LANES = 128
SUBLANES = 8
HALO = 32
HALO_OFF = HALO - (CONV_WIDTH - 1)
Q_BLK = 128
HEADS_PER_STEP = LANES // HEAD_DIM
SAMPLE_HEADS = 4
VMEM_LIMIT = 56 * 1024 * 1024

BF16 = jnp.bfloat16
F32 = jnp.float32


def _dot(a, b):
    return jnp.dot(a, b, preferred_element_type=F32)


def _dot_nt(a, b):
    return lax.dot_general(a, b, (((1,), (1,)), ((), ())), preferred_element_type=F32)


def _sigmoid(x):
    return 1.0 / (1.0 + jnp.exp(-x))


def _layer_norm(x, g, b):
    mu = jnp.mean(x, axis=-1, keepdims=True)
    xc = x - mu
    var = jnp.mean(xc * xc, axis=-1, keepdims=True)
    return xc * lax.rsqrt(var + LN_EPS) * g + b


def _merge_groups(outs, lses):
    mx = functools.reduce(jnp.maximum, lses)
    es = [jnp.exp(lv - mx) for lv in lses]
    den = functools.reduce(lambda a, b: a + b, es)
    num = functools.reduce(lambda a, b: a + b, [e * o for e, o in zip(es, outs)])
    return num / den


def _conv_layer_kernel(x_ref, buf_ref, win_ref, dww_ref, dwb_ref, lng_ref, lnb_ref, wout_ref,
                       ng_ref, nb_ref, o_ref, ob_ref, st_ref, ubuf, zbuf, cbuf, shbuf, *, tt, nt, rc):
    t = pl.program_id(1)

    @pl.when(t == 0)
    def _():
        ubuf[0:HALO, :] = buf_ref[...]

    x = x_ref[...]
    xb = x.astype(BF16)
    a = _dot(xb, win_ref[:, 0:E_CONV])
    gl = _dot(xb, win_ref[:, E_CONV:2 * E_CONV])
    ubuf[HALO:HALO + tt, :] = a * _sigmoid(gl)
    z = _dot(xb, win_ref[:, 2 * E_CONV:3 * E_CONV])
    zbuf[...] = z * _sigmoid(z)

    n_sh = tt + HALO - SUBLANES

    def lane_chunk(c, carry):
        lanes = pl.ds(pl.multiple_of(c * LANES, LANES), LANES)
        for s in range(1, SUBLANES):
            shbuf[s, :, :] = ubuf[s:s + n_sh, lanes]
        for r0 in range(0, tt, rc):
            acc = jnp.zeros((rc, LANES), F32)
            for k in range(CONV_WIDTH):
                lo = HALO_OFF + k + r0
                s = lo % SUBLANES
                base = lo - s
                assert base + rc <= (n_sh if s else tt + HALO)
                src = shbuf[s, base:base + rc, :] if s else ubuf[base:base + rc, lanes]
                acc = acc + dww_ref[k:k + 1, lanes] * src
            cbuf[r0:r0 + rc, lanes] = acc
        return carry

    lax.fori_loop(0, E_CONV // LANES, lane_chunk, 0)

    cn = _layer_norm(cbuf[...] + dwb_ref[...], lng_ref[...], lnb_ref[...])
    y = cn * _sigmoid(cn) * zbuf[...]
    f = _dot(y.astype(BF16), wout_ref[...])
    out = _layer_norm(ALPHA * x + f, ng_ref[...], nb_ref[...])
    o_ref[...] = out
    ob_ref[...] = out.astype(BF16)

    @pl.when(t == nt - 1)
    def _():
        st_ref[...] = ubuf[tt:tt + HALO, :]

    if nt > 1:
        ubuf[0:HALO, :] = ubuf[tt:tt + HALO, :]


def _conv_layer(x, buf, w_in, dw_w, dw_b, ln_g, ln_b, w_out, ng, nb):
    nbatch, seq, _ = x.shape
    tt = min(seq, 256)
    assert seq % tt == 0 and tt % SUBLANES == 0
    nt = seq // tt
    rc = min(tt, 32)
    buf_pad = jnp.pad(buf, ((0, 0), (HALO_OFF, 0), (0, 0)))
    row = lambda v: v.reshape(1, -1)
    const = lambda shape: pl.BlockSpec(shape, lambda b, t: (0,) * len(shape))
    out, out_b, st = pl.pallas_call(
        functools.partial(_conv_layer_kernel, tt=tt, nt=nt, rc=rc),
        grid=(nbatch, nt),
        in_specs=[
            pl.BlockSpec((None, tt, D_MODEL), lambda b, t: (b, t, 0)),
            pl.BlockSpec((None, HALO, E_CONV), lambda b, t: (b, 0, 0)),
            const((D_MODEL, 3 * E_CONV)),
            const((CONV_WIDTH, E_CONV)),
            const((1, E_CONV)), const((1, E_CONV)), const((1, E_CONV)),
            const((E_CONV, D_MODEL)),
            const((1, D_MODEL)), const((1, D_MODEL)),
        ],
        out_specs=[
            pl.BlockSpec((None, tt, D_MODEL), lambda b, t: (b, t, 0)),
            pl.BlockSpec((None, tt, D_MODEL), lambda b, t: (b, t, 0)),
            pl.BlockSpec((None, HALO, E_CONV), lambda b, t: (b, 0, 0)),
        ],
        out_shape=[
            jax.ShapeDtypeStruct((nbatch, seq, D_MODEL), F32),
            jax.ShapeDtypeStruct((nbatch, seq, D_MODEL), BF16),
            jax.ShapeDtypeStruct((nbatch, HALO, E_CONV), F32),
        ],
        scratch_shapes=[
            pltpu.VMEM((HALO + tt, E_CONV), F32),
            pltpu.VMEM((tt, E_CONV), F32),
            pltpu.VMEM((tt, E_CONV), F32),
            pltpu.VMEM((SUBLANES, tt + HALO - SUBLANES, LANES), F32),
        ],
        compiler_params=pltpu.CompilerParams(
            dimension_semantics=("arbitrary", "arbitrary"), vmem_limit_bytes=VMEM_LIMIT),
        name="conv_layer",
    )(x, buf_pad, w_in, dw_w, row(dw_b), row(ln_g), row(ln_b), w_out, row(ng), row(nb))
    return out, out_b, st[:, HALO_OFF:, :]


def _matmul_kernel(x_ref, w_ref, o_ref):
    o_ref[...] = _dot(x_ref[...].astype(BF16), w_ref[...])


def _matmul(x, w):
    m, k = x.shape
    n = w.shape[1]
    tm = min(m, 512)
    tn = 1024
    assert m % tm == 0 and n % tn == 0
    return pl.pallas_call(
        _matmul_kernel,
        grid=(m // tm, n // tn),
        in_specs=[pl.BlockSpec((tm, k), lambda i, j: (i, 0)),
                  pl.BlockSpec((k, tn), lambda i, j: (0, j))],
        out_specs=pl.BlockSpec((tm, tn), lambda i, j: (i, j)),
        out_shape=jax.ShapeDtypeStruct((m, n), F32),
        compiler_params=pltpu.CompilerParams(
            dimension_semantics=("arbitrary", "arbitrary"), vmem_limit_bytes=VMEM_LIMIT),
        name="attn_in_proj",
    )(x, w)


def _proj_ln_kernel(y_ref, x_ref, w_ref, g_ref, b_ref, o_ref):
    f = _dot(y_ref[...].astype(BF16), w_ref[...])
    o_ref[...] = _layer_norm(ALPHA * x_ref[...] + f, g_ref[...], b_ref[...])


def _proj_ln(y, x, w, g, b):
    m, k = y.shape
    n = w.shape[1]
    tm = min(m, 512)
    assert m % tm == 0
    return pl.pallas_call(
        _proj_ln_kernel,
        grid=(m // tm,),
        in_specs=[pl.BlockSpec((tm, k), lambda i: (i, 0)),
                  pl.BlockSpec((tm, n), lambda i: (i, 0)),
                  pl.BlockSpec((k, n), lambda i: (0, 0)),
                  pl.BlockSpec((1, n), lambda i: (0, 0)),
                  pl.BlockSpec((1, n), lambda i: (0, 0))],
        out_specs=pl.BlockSpec((tm, n), lambda i: (i, 0)),
        out_shape=jax.ShapeDtypeStruct((m, n), F32),
        compiler_params=pltpu.CompilerParams(
            dimension_semantics=("arbitrary",), vmem_limit_bytes=VMEM_LIMIT),
        name="attn_out_proj_ln",
    )(y, x, w, g.reshape(1, -1), b.reshape(1, -1))


N_COL_BLOCKS = 3 * N_GROUPS + 1
MM_ROWS = 512
MM_COLS = 2 * LANES
KV_T_COLS = 512


def _attn_prompt_kernel(*refs, seq):
    n_scr = N_COL_BLOCKS + 3 * N_GROUPS
    x_ref, w_ref = refs[0], refs[1]
    y_ref = refs[-(n_scr + N_GROUPS + 1)]
    kv_refs = refs[-(n_scr + N_GROUPS):-n_scr]
    h_scr = refs[-n_scr:-3 * N_GROUPS]
    o_scr = refs[-3 * N_GROUPS:-2 * N_GROUPS]
    m_scr = refs[-2 * N_GROUPS:-N_GROUPS]
    l_scr = refs[-N_GROUPS:]

    def project(i, carry):
        rs = pl.ds(pl.multiple_of(i * MM_ROWS, MM_ROWS), MM_ROWS)
        xb = x_ref[rs, :]
        for c in range(N_COL_BLOCKS * LANES // MM_COLS):
            r = _dot(xb, w_ref[:, c * MM_COLS:(c + 1) * MM_COLS])
            for j in range(MM_COLS // LANES):
                h_scr[c * (MM_COLS // LANES) + j][rs, :] = r[:, j * LANES:(j + 1) * LANES]
        return carry

    lax.fori_loop(0, seq // MM_ROWS, project, 0)

    for g, (window, _) in enumerate(GROUPS):
        keep = min(window, seq)
        ch = min(keep, KV_T_COLS)
        for kv in range(2):
            src = h_scr[3 * g + 1 + kv]
            for c in range(keep // ch):
                lo = seq - keep + c * ch
                kv_refs[g][kv, :, c * ch:(c + 1) * ch] = src[lo:lo + ch, :].T

    n_stack = HEADS_PER_STEP * Q_BLK
    lane = lax.broadcasted_iota(jnp.int32, (Q_BLK, LANES), 1)
    first_head = lane < HEAD_DIM
    qi = lax.broadcasted_iota(jnp.int32, (n_stack, Q_BLK), 0) & (Q_BLK - 1)
    kj = lax.broadcasted_iota(jnp.int32, (n_stack, Q_BLK), 1)
    bias_cur = jnp.where(kj <= qi, 0.0, NEG_INF)
    bias_prev = jnp.where(kj >= qi, 0.0, NEG_INF)
    bias_both = jnp.concatenate([bias_prev, bias_cur], axis=1)

    def rows(start, dil):
        if dil == 1:
            return pl.ds(start, Q_BLK)
        return pl.ds(start, Q_BLK, stride=dil)

    def unit_batch(g, dil, starts, has_prev, lead_start=None):
        q_ref, k_ref, v_ref = h_scr[3 * g: 3 * g + 3]
        ks = [k_ref[rows(st, dil), :].astype(BF16) for st in starts]
        vs = [v_ref[rows(st, dil), :].astype(BF16) for st in starts]
        if lead_start is not None:
            k_lead = k_ref[rows(lead_start, dil), :].astype(BF16)
            v_lead = v_ref[rows(lead_start, dil), :].astype(BF16)
        for a, st in enumerate(starts):
            q = q_ref[rows(st, dil), :] * QK_SCALE
            qs = jnp.concatenate([jnp.where(first_head, q, 0.0), jnp.where(first_head, 0.0, q)],
                                 axis=0).astype(BF16)
            if has_prev[a]:
                kp, vp = (ks[a - 1], vs[a - 1]) if a > 0 else (k_lead, v_lead)
                kk = jnp.concatenate([kp, ks[a]], axis=0)
                vv = jnp.concatenate([vp, vs[a]], axis=0)
                bias = bias_both
            else:
                kk, vv, bias = ks[a], vs[a], bias_cur
            s = _dot_nt(qs, kk) + bias
            m = jnp.max(s, axis=-1, keepdims=True)
            p = jnp.exp(s - m)
            l = jnp.sum(p, axis=-1, keepdims=True)
            o = _dot(p.astype(BF16), vv)
            o_scr[g][rows(st, dil), :] = jnp.where(first_head, o[0:Q_BLK], o[Q_BLK:])
            m_scr[g][rows(st, dil), :] = jnp.where(first_head, m[0:Q_BLK], m[Q_BLK:])
            l_scr[g][rows(st, dil), :] = jnp.where(first_head, l[0:Q_BLK], l[Q_BLK:])

    per_iter = 4
    for g, (_, dil) in enumerate(GROUPS):
        n_blk = seq // (dil * Q_BLK)
        step = dil * Q_BLK
        if n_blk == 1:
            def classes(i, carry, g=g, dil=dil):
                r0 = i * per_iter
                unit_batch(g, dil, [r0 + a for a in range(per_iter)], [False] * per_iter)
                return carry

            lax.fori_loop(0, dil // per_iter, classes, 0)
        else:
            assert n_blk % per_iter == 0

            def per_class(r, carry, g=g, dil=dil, n_blk=n_blk, step=step):
                unit_batch(g, dil, [r + a * step for a in range(per_iter)],
                           [False] + [True] * (per_iter - 1))

                def later(i, c2):
                    s0 = r + i * (per_iter * step)
                    unit_batch(g, dil, [s0 + a * step for a in range(per_iter)], [True] * per_iter,
                               lead_start=s0 - step)
                    return c2

                if n_blk > per_iter:
                    lax.fori_loop(1, n_blk // per_iter, later, 0)
                return carry

            lax.fori_loop(0, dil, per_class, 0)

    chunk = 256
    z_scr = h_scr[N_COL_BLOCKS - 1]

    def merge(i, carry):
        rs = pl.ds(pl.multiple_of(i * chunk, chunk), chunk)
        ms = [m_scr[g][rs, :] for g in range(N_GROUPS)]
        mx = functools.reduce(jnp.maximum, ms)
        es = [jnp.exp(mv - mx) for mv in ms]
        num = functools.reduce(lambda a, b: a + b, [es[g] * o_scr[g][rs, :] for g in range(N_GROUPS)])
        den = functools.reduce(lambda a, b: a + b, [es[g] * l_scr[g][rs, :] for g in range(N_GROUPS)])
        z = z_scr[rs, :]
        y_ref[rs, :] = (num / den) * (z * _sigmoid(z))
        return carry

    lax.fori_loop(0, seq // chunk, merge, 0)


def _attn_prompt(xb, w_pairs, li, n_layers, prevs):
    nbatch, seq, _ = xb.shape
    n_pair = w_pairs.shape[0]
    assert seq % (GROUPS[-1][1] * Q_BLK) == 0 and seq % MM_ROWS == 0
    keeps = [min(window, seq) for window, _ in GROUPS]

    def kv_spec(keep):
        return pl.BlockSpec((None, None, 2, LANES, keep), lambda b, hp: (li, b, 0, hp, 0))

    in_specs = [pl.BlockSpec((None, seq, D_MODEL), lambda b, hp: (b, 0, 0)),
                pl.BlockSpec((None, D_MODEL, N_COL_BLOCKS * LANES), lambda b, hp: (hp, 0, 0))]
    args = [xb, w_pairs]
    aliases = {}
    if prevs is not None:
        for g in range(N_GROUPS):
            aliases[len(args)] = 1 + g
            in_specs.append(pl.BlockSpec(memory_space=pl.ANY))
            args.append(prevs[g])
    res = pl.pallas_call(
        functools.partial(_attn_prompt_kernel, seq=seq),
        grid=(nbatch, n_pair),
        in_specs=in_specs,
        out_specs=[pl.BlockSpec((None, seq, LANES), lambda b, hp: (b, 0, hp))] + [kv_spec(k) for k in keeps],
        out_shape=[jax.ShapeDtypeStruct((nbatch, seq, E_ATTN), F32)]
        + [jax.ShapeDtypeStruct((n_layers, nbatch, 2, E_ATTN, k), F32) for k in keeps],
        scratch_shapes=[pltpu.VMEM((seq, LANES), F32) for _ in range(N_COL_BLOCKS + 3 * N_GROUPS)],
        input_output_aliases=aliases,
        compiler_params=pltpu.CompilerParams(
            dimension_semantics=("arbitrary", "arbitrary"), vmem_limit_bytes=VMEM_LIMIT),
        name="attn_prompt",
    )(*args)
    return res[0], list(res[1:])


def _pair_columns(w):
    d = w.shape[0]
    n_pair = E_ATTN // LANES
    return jnp.transpose(w.reshape(d, N_COL_BLOCKS, n_pair, LANES), (2, 0, 1, 3)).reshape(
        n_pair, d, N_COL_BLOCKS * LANES)


def _sample_biases(window, dil, t_new, n_heads):
    qi = np.arange(n_heads * t_new)[:, None] % t_new
    old_off = np.arange(window)[None, :] - window
    new_off = np.arange(LANES)[None, :] - (LANES - t_new)
    res = []
    for off, slot_ok in ((old_off, old_off < 0), (new_off, new_off >= 0)):
        dist = qi - off
        ok = slot_ok & (dist >= 0) & (dist <= window) & (dist % dil == 0)
        res.append(np.where(ok, 0.0, NEG_INF).astype(np.float32))
    return res


def _attn_sample_kernel(*refs, t_new):
    n_in = 10 + 3 * N_GROUPS
    h_refs = refs[:10]
    c_refs = refs[10:10 + N_GROUPS]
    b_refs = refs[10 + N_GROUPS:n_in]
    y_ref = refs[-(N_GROUPS + 1)]
    o_refs = refs[-N_GROUPS:]

    width = SAMPLE_HEADS * HEAD_DIM
    n_rows = SAMPLE_HEADS * t_new
    row = lax.broadcasted_iota(jnp.int32, (n_rows, width), 0)
    col = lax.broadcasted_iota(jnp.int32, (n_rows, width), 1)
    diag = jnp.right_shift(row, t_new.bit_length() - 1) == jnp.right_shift(col, HEAD_DIM.bit_length() - 1)
    new_lanes = lax.broadcasted_iota(jnp.int32, (width, LANES), 1) >= LANES - t_new
    zpad = jnp.zeros((LANES - t_new, width), F32)

    outs, lses = [], []
    for g, (window, _) in enumerate(GROUPS):
        q = h_refs[3 * g][...] * QK_SCALE
        kt_new = jnp.concatenate([zpad, h_refs[3 * g + 1][...]], axis=0).T
        vt_new = jnp.concatenate([zpad, h_refs[3 * g + 2][...]], axis=0).T
        kt = c_refs[g][0]
        vt = c_refs[g][1]
        for idx, (old, new) in enumerate(((kt, kt_new), (vt, vt_new))):
            rolled = pltpu.roll(old, window - t_new, axis=1)
            if window > LANES:
                o_refs[g][idx, :, 0:window - LANES] = rolled[:, 0:window - LANES]
            o_refs[g][idx, :, window - LANES:window] = jnp.where(new_lanes, new, rolled[:, window - LANES:])

        q_rep = jnp.broadcast_to(q[None], (SAMPLE_HEADS, t_new, width)).reshape(n_rows, width)
        q_bd = jnp.where(diag, q_rep, 0.0).astype(BF16)
        s_old = _dot(q_bd, kt.astype(BF16)) + b_refs[2 * g][...]
        s_new = _dot(q_bd, kt_new.astype(BF16)) + b_refs[2 * g + 1][...]
        m = jnp.maximum(jnp.max(s_old, axis=-1, keepdims=True), jnp.max(s_new, axis=-1, keepdims=True))
        p_old = jnp.exp(s_old - m)
        p_new = jnp.exp(s_new - m)
        l = jnp.sum(p_old, axis=-1, keepdims=True) + jnp.sum(p_new, axis=-1, keepdims=True)
        o = (_dot_nt(p_old.astype(BF16), vt.astype(BF16))
             + _dot_nt(p_new.astype(BF16), vt_new.astype(BF16))) * (1.0 / l)
        lse = m + jnp.log(l)
        outs.append(jnp.where(diag, o, 0.0).reshape(SAMPLE_HEADS, t_new, width).sum(axis=0))
        lses.append(jnp.where(diag, lse, 0.0).reshape(SAMPLE_HEADS, t_new, width).sum(axis=0))

    z = h_refs[9][...]
    y_ref[...] = _merge_groups(outs, lses) * (z * _sigmoid(z))


def _attn_sample(h, caches_t, li, prevs):
    nbatch, t_new, _ = h.shape
    n_layers = caches_t[0].shape[0]
    width = SAMPLE_HEADS * HEAD_DIM
    n_chunk = E_ATTN // width
    assert t_new & (t_new - 1) == 0 and t_new <= SUBLANES

    def h_spec(blk):
        return pl.BlockSpec((None, t_new, width), lambda b, c: (b, 0, blk * n_chunk + c))

    def cache_spec(window):
        return pl.BlockSpec((None, None, 2, width, window), lambda b, c: (li, b, 0, c, 0))

    biases = []
    for window, dil in GROUPS:
        biases += [jnp.asarray(a) for a in _sample_biases(window, dil, t_new, SAMPLE_HEADS)]
    full = lambda a: pl.BlockSpec(a.shape, lambda b, c: (0, 0))

    in_specs = [h_spec(blk) for blk in range(10)]
    in_specs += [cache_spec(window) for window, _ in GROUPS]
    in_specs += [full(a) for a in biases]
    args = [h] * 10 + list(caches_t) + biases
    aliases = {}
    if prevs is not None:
        for g in range(N_GROUPS):
            aliases[len(args)] = 1 + g
            in_specs.append(pl.BlockSpec(memory_space=pl.ANY))
            args.append(prevs[g])
    res = pl.pallas_call(
        functools.partial(_attn_sample_kernel, t_new=t_new),
        grid=(nbatch, n_chunk),
        in_specs=in_specs,
        out_specs=[pl.BlockSpec((None, t_new, width), lambda b, c: (b, 0, c))]
        + [cache_spec(window) for window, _ in GROUPS],
        out_shape=[jax.ShapeDtypeStruct((nbatch, t_new, E_ATTN), F32)]
        + [jax.ShapeDtypeStruct((n_layers, nbatch, 2, E_ATTN, window), F32) for window, _ in GROUPS],
        input_output_aliases=aliases,
        compiler_params=pltpu.CompilerParams(
            dimension_semantics=("arbitrary", "arbitrary"), vmem_limit_bytes=VMEM_LIMIT),
        name="attn_sample",
    )(*args)
    return res[0], list(res[1:])


def _cache_time_minor(c):
    n_layers, nbatch, window = c.shape[:3]
    return jnp.transpose(c, (0, 1, 3, 4, 5, 2)).reshape(n_layers, nbatch, 2, E_ATTN, window)


def _cache_time_major(c):
    n_layers, nbatch, window = c.shape[0], c.shape[1], c.shape[-1]
    c = c.reshape(n_layers, nbatch, 2, N_HEADS, HEAD_DIM, window)
    return jnp.transpose(c, (0, 1, 5, 2, 3, 4))


def kernel(x_prompt, x_sample, cache_kv_w128_d1, cache_kv_w512_d4, cache_kv_w2048_d16, state_conv,
           conv_w_in, conv_dw_w, conv_dw_b, conv_ln_g, conv_ln_b, conv_w_out,
           attn_w_in, attn_w_out, norm_g, norm_b):
    caches_t = [_cache_time_minor(c) for c in (cache_kv_w128_d1, cache_kv_w512_d4, cache_kv_w2048_d16)]
    n_attn = attn_w_in.shape[0]
    conv_w_in_b = conv_w_in.astype(BF16)
    conv_w_out_b = conv_w_out.astype(BF16)
    attn_w_in_b = attn_w_in.astype(BF16)
    attn_w_out_b = attn_w_out.astype(BF16)

    yp, ys = x_prompt, x_sample
    bp, tp, _ = yp.shape
    bs, ts, _ = ys.shape
    kv_p = None
    kv_s = None
    conv_p, conv_s = [], []
    for i in range(DEPTH):
        li = i // 2
        if i % 2 == 0:
            prm = (conv_w_in_b[li], conv_dw_w[li], conv_dw_b[li], conv_ln_g[li], conv_ln_b[li],
                   conv_w_out_b[li], norm_g[i], norm_b[i])
            zero_buf = jnp.zeros((bp, CONV_WIDTH - 1, E_CONV), F32)
            yp, yp_b, stp = _conv_layer(yp, zero_buf, *prm)
            ys, _, sts = _conv_layer(ys, state_conv[li], *prm)
            conv_p.append(stp)
            conv_s.append(sts)
        else:
            hs = _matmul(ys.reshape(bs * ts, D_MODEL), attn_w_in_b[li]).reshape(bs, ts, -1)
            op, kv_p = _attn_prompt(yp_b, _pair_columns(attn_w_in_b[li]), li, n_attn, kv_p)
            os_, kv_s = _attn_sample(hs, caches_t, li, kv_s)
            yp = _proj_ln(op.reshape(bp * tp, E_ATTN), yp.reshape(bp * tp, D_MODEL),
                          attn_w_out_b[li], norm_g[i], norm_b[i]).reshape(bp, tp, D_MODEL)
            ys = _proj_ln(os_.reshape(bs * ts, E_ATTN), ys.reshape(bs * ts, D_MODEL),
                          attn_w_out_b[li], norm_g[i], norm_b[i]).reshape(bs, ts, D_MODEL)
    return (yp, ys,
            _cache_time_major(kv_p[0]), _cache_time_major(kv_p[1]), _cache_time_major(kv_p[2]),
            jnp.stack(conv_p),
            _cache_time_major(kv_s[0]), _cache_time_major(kv_s[1]), _cache_time_major(kv_s[2]),
            jnp.stack(conv_s))
```

```python
import functools

import jax
import jax.numpy as jnp
import numpy as np
from jax import lax
from jax.experimental import pallas as pl
from jax.experimental.pallas import tpu as pltpu

D_MODEL = 1024
DEPTH = 4
E_CONV = D_MODEL
CONV_WIDTH = 31
HEAD_DIM = 64
N_HEADS = D_MODEL // HEAD_DIM
E_ATTN = N_HEADS * HEAD_DIM
GROUPS = ((128, 1), (512, 4), (2048, 16))
N_GROUPS = len(GROUPS)
ALPHA = (2.0 * DEPTH) ** 0.25
LN_EPS = 1e-5
NEG_INF = -1e30
QK_SCALE = HEAD_DIM ** -0.5

LANES = 128
SUBLANES = 8
HALO = 32
HALO_OFF = HALO - (CONV_WIDTH - 1)
Q_BLK = 128
HEADS_PER_STEP = LANES // HEAD_DIM
SAMPLE_HEADS = 4
VMEM_LIMIT = 56 * 1024 * 1024

BF16 = jnp.bfloat16
F32 = jnp.float32


def _dot(a, b):
    return jnp.dot(a, b, preferred_element_type=F32)


def _dot_nt(a, b):
    return lax.dot_general(a, b, (((1,), (1,)), ((), ())), preferred_element_type=F32)


def _sigmoid(x):
    return 1.0 / (1.0 + jnp.exp(-x))


def _layer_norm(x, g, b):
    mu = jnp.mean(x, axis=-1, keepdims=True)
    xc = x - mu
    var = jnp.mean(xc * xc, axis=-1, keepdims=True)
    return xc * lax.rsqrt(var + LN_EPS) * g + b


def _merge_groups(outs, lses):
    mx = functools.reduce(jnp.maximum, lses)
    es = [jnp.exp(lv - mx) for lv in lses]
    den = functools.reduce(lambda a, b: a + b, es)
    num = functools.reduce(lambda a, b: a + b, [e * o for e, o in zip(es, outs)])
    return num / den


def _conv_layer_kernel(x_ref, buf_ref, win_ref, dww_ref, dwb_ref, lng_ref, lnb_ref, wout_ref,
                       ng_ref, nb_ref, o_ref, ob_ref, st_ref, ubuf, zbuf, cbuf, shbuf, *, tt, nt, rc):
    t = pl.program_id(1)

    @pl.when(t == 0)
    def _():
        ubuf[0:HALO, :] = buf_ref[...]

    x = x_ref[...]
    xb = x.astype(BF16)
    a = _dot(xb, win_ref[:, 0:E_CONV])
    gl = _dot(xb, win_ref[:, E_CONV:2 * E_CONV])
    ubuf[HALO:HALO + tt, :] = a * _sigmoid(gl)
    z = _dot(xb, win_ref[:, 2 * E_CONV:3 * E_CONV])
    zbuf[...] = z * _sigmoid(z)

    n_sh = tt + HALO - SUBLANES

    def lane_chunk(c, carry):
        lanes = pl.ds(pl.multiple_of(c * LANES, LANES), LANES)
        for s in range(1, SUBLANES):
            shbuf[s, :, :] = ubuf[s:s + n_sh, lanes]
        for r0 in range(0, tt, rc):
            acc = jnp.zeros((rc, LANES), F32)
            for k in range(CONV_WIDTH):
                lo = HALO_OFF + k + r0
                s = lo % SUBLANES
                base = lo - s
                assert base + rc <= (n_sh if s else tt + HALO)
                src = shbuf[s, base:base + rc, :] if s else ubuf[base:base + rc, lanes]
                acc = acc + dww_ref[k:k + 1, lanes] * src
            cbuf[r0:r0 + rc, lanes] = acc
        return carry

    lax.fori_loop(0, E_CONV // LANES, lane_chunk, 0)

    cn = _layer_norm(cbuf[...] + dwb_ref[...], lng_ref[...], lnb_ref[...])
    y = cn * _sigmoid(cn) * zbuf[...]
    f = _dot(y.astype(BF16), wout_ref[...])
    out = _layer_norm(ALPHA * x + f, ng_ref[...], nb_ref[...])
    o_ref[...] = out
    ob_ref[...] = out.astype(BF16)

    @pl.when(t == nt - 1)
    def _():
        st_ref[...] = ubuf[tt:tt + HALO, :]

    if nt > 1:
        ubuf[0:HALO, :] = ubuf[tt:tt + HALO, :]


def _conv_layer(x, buf, w_in, dw_w, dw_b, ln_g, ln_b, w_out, ng, nb):
    nbatch, seq, _ = x.shape
    tt = min(seq, 256)
    assert seq % tt == 0 and tt % SUBLANES == 0
    nt = seq // tt
    rc = min(tt, 32)
    buf_pad = jnp.pad(buf, ((0, 0), (HALO_OFF, 0), (0, 0)))
    row = lambda v: v.reshape(1, -1)
    const = lambda shape: pl.BlockSpec(shape, lambda b, t: (0,) * len(shape))
    out, out_b, st = pl.pallas_call(
        functools.partial(_conv_layer_kernel, tt=tt, nt=nt, rc=rc),
        grid=(nbatch, nt),
        in_specs=[
            pl.BlockSpec((None, tt, D_MODEL), lambda b, t: (b, t, 0)),
            pl.BlockSpec((None, HALO, E_CONV), lambda b, t: (b, 0, 0)),
            const((D_MODEL, 3 * E_CONV)),
            const((CONV_WIDTH, E_CONV)),
            const((1, E_CONV)), const((1, E_CONV)), const((1, E_CONV)),
            const((E_CONV, D_MODEL)),
            const((1, D_MODEL)), const((1, D_MODEL)),
        ],
        out_specs=[
            pl.BlockSpec((None, tt, D_MODEL), lambda b, t: (b, t, 0)),
            pl.BlockSpec((None, tt, D_MODEL), lambda b, t: (b, t, 0)),
            pl.BlockSpec((None, HALO, E_CONV), lambda b, t: (b, 0, 0)),
        ],
        out_shape=[
            jax.ShapeDtypeStruct((nbatch, seq, D_MODEL), F32),
            jax.ShapeDtypeStruct((nbatch, seq, D_MODEL), BF16),
            jax.ShapeDtypeStruct((nbatch, HALO, E_CONV), F32),
        ],
        scratch_shapes=[
            pltpu.VMEM((HALO + tt, E_CONV), F32),
            pltpu.VMEM((tt, E_CONV), F32),
            pltpu.VMEM((tt, E_CONV), F32),
            pltpu.VMEM((SUBLANES, tt + HALO - SUBLANES, LANES), F32),
        ],
        compiler_params=pltpu.CompilerParams(
            dimension_semantics=("arbitrary", "arbitrary"), vmem_limit_bytes=VMEM_LIMIT),
        name="conv_layer",
    )(x, buf_pad, w_in, dw_w, row(dw_b), row(ln_g), row(ln_b), w_out, row(ng), row(nb))
    return out, out_b, st[:, HALO_OFF:, :]


def _matmul_kernel(x_ref, w_ref, o_ref):
    o_ref[...] = _dot(x_ref[...].astype(BF16), w_ref[...].astype(BF16))


def _matmul(x, w, li):
    m, k = x.shape
    n = w.shape[2]
    tm = min(m, 512)
    tn = 1024
    assert m % tm == 0 and n % tn == 0
    return pl.pallas_call(
        _matmul_kernel,
        grid=(m // tm, n // tn),
        in_specs=[pl.BlockSpec((tm, k), lambda i, j: (i, 0)),
                  pl.BlockSpec((None, k, tn), lambda i, j: (li, 0, j))],
        out_specs=pl.BlockSpec((tm, tn), lambda i, j: (i, j)),
        out_shape=jax.ShapeDtypeStruct((m, n), F32),
        compiler_params=pltpu.CompilerParams(
            dimension_semantics=("arbitrary", "arbitrary"), vmem_limit_bytes=VMEM_LIMIT),
        name="attn_in_proj",
    )(x, w)


def _proj_ln_kernel(y_ref, x_ref, w_ref, g_ref, b_ref, o_ref):
    f = _dot(y_ref[...].astype(BF16), w_ref[...])
    o_ref[...] = _layer_norm(ALPHA * x_ref[...] + f, g_ref[...], b_ref[...])


def _proj_ln(y, x, w, g, b):
    m, k = y.shape
    n = w.shape[1]
    tm = min(m, 512)
    assert m % tm == 0
    return pl.pallas_call(
        _proj_ln_kernel,
        grid=(m // tm,),
        in_specs=[pl.BlockSpec((tm, k), lambda i: (i, 0)),
                  pl.BlockSpec((tm, n), lambda i: (i, 0)),
                  pl.BlockSpec((k, n), lambda i: (0, 0)),
                  pl.BlockSpec((1, n), lambda i: (0, 0)),
                  pl.BlockSpec((1, n), lambda i: (0, 0))],
        out_specs=pl.BlockSpec((tm, n), lambda i: (i, 0)),
        out_shape=jax.ShapeDtypeStruct((m, n), F32),
        compiler_params=pltpu.CompilerParams(
            dimension_semantics=("arbitrary",), vmem_limit_bytes=VMEM_LIMIT),
        name="attn_out_proj_ln",
    )(y, x, w, g.reshape(1, -1), b.reshape(1, -1))


N_COL_BLOCKS = 3 * N_GROUPS + 1
MM_ROWS = 512
MM_COLS = 2 * LANES
KV_T_COLS = 512
UNITS_PER_ITER = 8


def _attn_prompt_kernel(*refs, seq):
    n_scr = N_COL_BLOCKS + 3 * N_GROUPS
    x_ref, w_ref = refs[0], refs[1]
    y_ref = refs[-(n_scr + N_GROUPS + 1)]
    kv_refs = refs[-(n_scr + N_GROUPS):-n_scr]
    h_scr = refs[-n_scr:-3 * N_GROUPS]
    o_scr = refs[-3 * N_GROUPS:-2 * N_GROUPS]
    m_scr = refs[-2 * N_GROUPS:-N_GROUPS]
    l_scr = refs[-N_GROUPS:]

    def project(i, carry):
        rs = pl.ds(pl.multiple_of(i * MM_ROWS, MM_ROWS), MM_ROWS)
        xb = x_ref[rs, :]
        for c in range(N_COL_BLOCKS * LANES // MM_COLS):
            r = _dot(xb, w_ref[:, c * MM_COLS:(c + 1) * MM_COLS])
            for j in range(MM_COLS // LANES):
                h_scr[c * (MM_COLS // LANES) + j][rs, :] = r[:, j * LANES:(j + 1) * LANES]
        return carry

    lax.fori_loop(0, seq // MM_ROWS, project, 0)

    for g, (window, _) in enumerate(GROUPS):
        keep = min(window, seq)
        ch = min(keep, KV_T_COLS)
        for kv in range(2):
            src = h_scr[3 * g + 1 + kv]
            for c in range(keep // ch):
                lo = seq - keep + c * ch
                kv_refs[g][kv, :, c * ch:(c + 1) * ch] = src[lo:lo + ch, :].T

    n_stack = HEADS_PER_STEP * Q_BLK
    lane = lax.broadcasted_iota(jnp.int32, (Q_BLK, LANES), 1)
    first_head = lane < HEAD_DIM
    qi = lax.broadcasted_iota(jnp.int32, (n_stack, Q_BLK), 0) & (Q_BLK - 1)
    kj = lax.broadcasted_iota(jnp.int32, (n_stack, Q_BLK), 1)
    bias_cur = jnp.where(kj <= qi, 0.0, NEG_INF)
    bias_prev = jnp.where(kj >= qi, 0.0, NEG_INF)
    bias_both = jnp.concatenate([bias_prev, bias_cur], axis=1)
    key_row = lax.broadcasted_iota(jnp.int32, (n_stack, LANES), 0)
    key_lane = lax.broadcasted_iota(jnp.int32, (n_stack, LANES), 1)
    ones_all = jnp.ones((n_stack, LANES), BF16)
    ones_head = jnp.where((key_row < Q_BLK) == (key_lane < HEAD_DIM), 1.0, 0.0).astype(BF16)

    def rows(start, dil):
        if dil == 1:
            return pl.ds(start, Q_BLK)
        return pl.ds(start, Q_BLK, stride=dil)

    def unit_batch(g, dil, chains):
        q_ref, k_ref, v_ref = h_scr[3 * g: 3 * g + 3]
        for starts, lead_start in chains:
            if lead_start is None:
                k0 = k_ref[rows(starts[0], dil), :]
                v0 = v_ref[rows(starts[0], dil), :]
                k_bd = jnp.concatenate([jnp.where(first_head, k0, 0.0), jnp.where(first_head, 0.0, k0)],
                                       axis=0).astype(BF16)
                v_bd = jnp.concatenate([jnp.where(first_head, v0, 0.0), jnp.where(first_head, 0.0, v0)],
                                       axis=0).astype(BF16)
                v_bd = jnp.concatenate([v_bd, ones_head], axis=1)
                ks, vs = [k0.astype(BF16)], [v0.astype(BF16)]
            else:
                ks, vs = [], []
                k_lead = k_ref[rows(lead_start, dil), :].astype(BF16)
                v_lead = v_ref[rows(lead_start, dil), :].astype(BF16)
            ks += [k_ref[rows(st, dil), :].astype(BF16) for st in starts[len(ks):]]
            vs += [v_ref[rows(st, dil), :].astype(BF16) for st in starts[len(vs):]]
            for a, st in enumerate(starts):
                q = q_ref[rows(st, dil), :] * QK_SCALE
                if a > 0 or lead_start is not None:
                    qs = jnp.concatenate([jnp.where(first_head, q, 0.0), jnp.where(first_head, 0.0, q)],
                                         axis=0).astype(BF16)
                    kp, vp = (ks[a - 1], vs[a - 1]) if a > 0 else (k_lead, v_lead)
                    s = _dot_nt(qs, jnp.concatenate([kp, ks[a]], axis=0)) + bias_both
                    m = jnp.max(s, axis=-1, keepdims=True)
                    p = jnp.exp(s - m)
                    vv = jnp.concatenate([jnp.concatenate([vp, vs[a]], axis=0), ones_all], axis=1)
                    ol = _dot(p.astype(BF16), vv)
                    o = jnp.where(first_head, ol[0:Q_BLK, 0:LANES], ol[Q_BLK:, 0:LANES])
                    l = jnp.where(first_head, ol[0:Q_BLK, LANES:], ol[Q_BLK:, LANES:])
                    m0, m1 = m[0:Q_BLK], m[Q_BLK:]
                else:
                    s = _dot_nt(q.astype(BF16), k_bd)
                    s0 = s[:, 0:Q_BLK] + bias_cur[0:Q_BLK]
                    s1 = s[:, Q_BLK:] + bias_cur[0:Q_BLK]
                    m0 = jnp.max(s0, axis=-1, keepdims=True)
                    m1 = jnp.max(s1, axis=-1, keepdims=True)
                    p0 = jnp.exp(s0 - m0)
                    p1 = jnp.exp(s1 - m1)
                    ol = _dot(jnp.concatenate([p0, p1], axis=1).astype(BF16), v_bd)
                    o, l = ol[:, 0:LANES], ol[:, LANES:]
                o_scr[g][rows(st, dil), :] = o
                m_scr[g][rows(st, dil), :] = jnp.where(first_head, m0, m1)
                l_scr[g][rows(st, dil), :] = l

    for g, (_, dil) in enumerate(GROUPS):
        n_blk = seq // (dil * Q_BLK)
        step = dil * Q_BLK
        if n_blk <= UNITS_PER_ITER:
            n_cls = UNITS_PER_ITER // n_blk
            assert dil % n_cls == 0

            def classes(i, carry, g=g, dil=dil, n_blk=n_blk, step=step, n_cls=n_cls):
                unit_batch(g, dil, [([i * n_cls + c + a * step for a in range(n_blk)], None)
                                    for c in range(n_cls)])
                return carry

            lax.fori_loop(0, dil // n_cls, classes, 0)
        else:
            assert n_blk % UNITS_PER_ITER == 0

            def per_class(r, carry, g=g, dil=dil, n_blk=n_blk, step=step):
                unit_batch(g, dil, [([r + a * step for a in range(UNITS_PER_ITER)], None)])

                def later(i, c2):
                    s0 = r + i * (UNITS_PER_ITER * step)
                    unit_batch(g, dil, [([s0 + a * step for a in range(UNITS_PER_ITER)], s0 - step)])
                    return c2

                lax.fori_loop(1, n_blk // UNITS_PER_ITER, later, 0)
                return carry

            lax.fori_loop(0, dil, per_class, 0)

    chunk = 256
    z_scr = h_scr[N_COL_BLOCKS - 1]

    def merge(i, carry):
        rs = pl.ds(pl.multiple_of(i * chunk, chunk), chunk)
        ms = [m_scr[g][rs, :] for g in range(N_GROUPS)]
        mx = functools.reduce(jnp.maximum, ms)
        es = [jnp.exp(mv - mx) for mv in ms]
        num = functools.reduce(lambda a, b: a + b, [es[g] * o_scr[g][rs, :] for g in range(N_GROUPS)])
        den = functools.reduce(lambda a, b: a + b, [es[g] * l_scr[g][rs, :] for g in range(N_GROUPS)])
        z = z_scr[rs, :]
        y_ref[rs, :] = (num / den) * (z * _sigmoid(z))
        return carry

    lax.fori_loop(0, seq // chunk, merge, 0)


def _attn_prompt(xb, w_pairs, li, prevs):
    nbatch, seq, _ = xb.shape
    n_layers, n_pair = w_pairs.shape[:2]
    assert seq % (GROUPS[-1][1] * Q_BLK) == 0 and seq % MM_ROWS == 0
    keeps = [min(window, seq) for window, _ in GROUPS]

    def kv_spec(keep):
        return pl.BlockSpec((None, None, 2, LANES, keep), lambda b, hp: (li, b, 0, hp, 0))

    in_specs = [pl.BlockSpec((None, seq, D_MODEL), lambda b, hp: (b, 0, 0)),
                pl.BlockSpec((None, None, D_MODEL, N_COL_BLOCKS * LANES), lambda b, hp: (li, hp, 0, 0))]
    args = [xb, w_pairs]
    aliases = {}
    if prevs is not None:
        for g in range(N_GROUPS):
            aliases[len(args)] = 1 + g
            in_specs.append(pl.BlockSpec(memory_space=pl.ANY))
            args.append(prevs[g])
    res = pl.pallas_call(
        functools.partial(_attn_prompt_kernel, seq=seq),
        grid=(nbatch, n_pair),
        in_specs=in_specs,
        out_specs=[pl.BlockSpec((None, seq, LANES), lambda b, hp: (b, 0, hp))] + [kv_spec(k) for k in keeps],
        out_shape=[jax.ShapeDtypeStruct((nbatch, seq, E_ATTN), F32)]
        + [jax.ShapeDtypeStruct((n_layers, nbatch, 2, E_ATTN, k), F32) for k in keeps],
        scratch_shapes=[pltpu.VMEM((seq, LANES), F32) for _ in range(N_COL_BLOCKS + 3 * N_GROUPS)],
        input_output_aliases=aliases,
        compiler_params=pltpu.CompilerParams(
            dimension_semantics=("arbitrary", "arbitrary"), vmem_limit_bytes=VMEM_LIMIT),
        name="attn_prompt",
    )(*args)
    return res[0], list(res[1:])


def _pair_columns(w):
    n_layers, d, _ = w.shape
    n_pair = E_ATTN // LANES
    w = jnp.transpose(w.reshape(n_layers, d, N_COL_BLOCKS, n_pair, LANES), (0, 3, 1, 2, 4))
    return w.reshape(n_layers, n_pair, d, N_COL_BLOCKS * LANES).astype(BF16)


def _sample_biases(window, dil, t_new, n_heads):
    qi = np.arange(n_heads * t_new)[:, None] % t_new
    old_off = np.arange(window)[None, :] - window
    new_off = np.arange(LANES)[None, :] - (LANES - t_new)
    res = []
    for off, slot_ok in ((old_off, old_off < 0), (new_off, new_off >= 0)):
        dist = qi - off
        ok = slot_ok & (dist >= 0) & (dist <= window) & (dist % dil == 0)
        res.append(np.where(ok, 0.0, NEG_INF).astype(np.float32))
    return res


def _attn_sample_kernel(*refs, t_new):
    n_in = 10 + 3 * N_GROUPS
    h_refs = refs[:10]
    c_refs = refs[10:10 + N_GROUPS]
    b_refs = refs[10 + N_GROUPS:n_in]
    y_ref = refs[-(N_GROUPS + 1)]
    o_refs = refs[-N_GROUPS:]

    width = SAMPLE_HEADS * HEAD_DIM
    n_rows = SAMPLE_HEADS * t_new
    row = lax.broadcasted_iota(jnp.int32, (n_rows, width), 0)
    col = lax.broadcasted_iota(jnp.int32, (n_rows, width), 1)
    diag = jnp.right_shift(row, t_new.bit_length() - 1) == jnp.right_shift(col, HEAD_DIM.bit_length() - 1)
    new_lanes = lax.broadcasted_iota(jnp.int32, (width, LANES), 1) >= LANES - t_new
    zpad = jnp.zeros((LANES - t_new, width), F32)

    outs, lses = [], []
    for g, (window, _) in enumerate(GROUPS):
        q = h_refs[3 * g][...] * QK_SCALE
        kt_new = jnp.concatenate([zpad, h_refs[3 * g + 1][...]], axis=0).T
        vt_new = jnp.concatenate([zpad, h_refs[3 * g + 2][...]], axis=0).T
        kt = c_refs[g][0]
        vt = c_refs[g][1]
        for idx, (old, new) in enumerate(((kt, kt_new), (vt, vt_new))):
            rolled = pltpu.roll(old, window - t_new, axis=1)
            if window > LANES:
                o_refs[g][idx, :, 0:window - LANES] = rolled[:, 0:window - LANES]
            o_refs[g][idx, :, window - LANES:window] = jnp.where(new_lanes, new, rolled[:, window - LANES:])

        q_rep = jnp.broadcast_to(q[None], (SAMPLE_HEADS, t_new, width)).reshape(n_rows, width)
        q_bd = jnp.where(diag, q_rep, 0.0).astype(BF16)
        s_old = _dot(q_bd, kt.astype(BF16)) + b_refs[2 * g][...]
        s_new = _dot(q_bd, kt_new.astype(BF16)) + b_refs[2 * g + 1][...]
        m = jnp.maximum(jnp.max(s_old, axis=-1, keepdims=True), jnp.max(s_new, axis=-1, keepdims=True))
        p_old = jnp.exp(s_old - m)
        p_new = jnp.exp(s_new - m)
        l = jnp.sum(p_old, axis=-1, keepdims=True) + jnp.sum(p_new, axis=-1, keepdims=True)
        o = (_dot_nt(p_old.astype(BF16), vt.astype(BF16))
             + _dot_nt(p_new.astype(BF16), vt_new.astype(BF16))) * (1.0 / l)
        lse = m + jnp.log(l)
        outs.append(jnp.where(diag, o, 0.0).reshape(SAMPLE_HEADS, t_new, width).sum(axis=0))
        lses.append(jnp.where(diag, lse, 0.0).reshape(SAMPLE_HEADS, t_new, width).sum(axis=0))

    z = h_refs[9][...]
    y_ref[...] = _merge_groups(outs, lses) * (z * _sigmoid(z))


def _attn_sample(h, caches_t, li, prevs):
    nbatch, t_new, _ = h.shape
    n_layers = caches_t[0].shape[0]
    width = SAMPLE_HEADS * HEAD_DIM
    n_chunk = E_ATTN // width
    assert t_new & (t_new - 1) == 0 and t_new <= SUBLANES

    def h_spec(blk):
        return pl.BlockSpec((None, t_new, width), lambda b, c: (b, 0, blk * n_chunk + c))

    def cache_spec(window):
        return pl.BlockSpec((None, None, 2, width, window), lambda b, c: (li, b, 0, c, 0))

    biases = []
    for window, dil in GROUPS:
        biases += [jnp.asarray(a) for a in _sample_biases(window, dil, t_new, SAMPLE_HEADS)]
    full = lambda a: pl.BlockSpec(a.shape, lambda b, c: (0, 0))

    in_specs = [h_spec(blk) for blk in range(10)]
    in_specs += [cache_spec(window) for window, _ in GROUPS]
    in_specs += [full(a) for a in biases]
    args = [h] * 10 + list(caches_t) + biases
    aliases = {}
    if prevs is not None:
        for g in range(N_GROUPS):
            aliases[len(args)] = 1 + g
            in_specs.append(pl.BlockSpec(memory_space=pl.ANY))
            args.append(prevs[g])
    res = pl.pallas_call(
        functools.partial(_attn_sample_kernel, t_new=t_new),
        grid=(nbatch, n_chunk),
        in_specs=in_specs,
        out_specs=[pl.BlockSpec((None, t_new, width), lambda b, c: (b, 0, c))]
        + [cache_spec(window) for window, _ in GROUPS],
        out_shape=[jax.ShapeDtypeStruct((nbatch, t_new, E_ATTN), F32)]
        + [jax.ShapeDtypeStruct((n_layers, nbatch, 2, E_ATTN, window), F32) for window, _ in GROUPS],
        input_output_aliases=aliases,
        compiler_params=pltpu.CompilerParams(
            dimension_semantics=("arbitrary", "arbitrary"), vmem_limit_bytes=VMEM_LIMIT),
        name="attn_sample",
    )(*args)
    return res[0], list(res[1:])


def _cache_time_minor(c):
    n_layers, nbatch, window = c.shape[:3]
    return jnp.transpose(c, (0, 1, 3, 4, 5, 2)).reshape(n_layers, nbatch, 2, E_ATTN, window)


def _cache_time_major(c):
    n_layers, nbatch, window = c.shape[0], c.shape[1], c.shape[-1]
    c = c.reshape(n_layers, nbatch, 2, N_HEADS, HEAD_DIM, window)
    return jnp.transpose(c, (0, 1, 5, 2, 3, 4))


def kernel(x_prompt, x_sample, cache_kv_w128_d1, cache_kv_w512_d4, cache_kv_w2048_d16, state_conv,
           conv_w_in, conv_dw_w, conv_dw_b, conv_ln_g, conv_ln_b, conv_w_out,
           attn_w_in, attn_w_out, norm_g, norm_b):
    caches_t = [_cache_time_minor(c) for c in (cache_kv_w128_d1, cache_kv_w512_d4, cache_kv_w2048_d16)]
    conv_w_in_b = conv_w_in.astype(BF16)
    conv_w_out_b = conv_w_out.astype(BF16)
    attn_w_pairs = _pair_columns(attn_w_in)
    attn_w_out_b = attn_w_out.astype(BF16)

    yp, ys = x_prompt, x_sample
    bp, tp, _ = yp.shape
    bs, ts, _ = ys.shape
    kv_p = None
    kv_s = None
    conv_p, conv_s = [], []
    for i in range(DEPTH):
        li = i // 2
        if i % 2 == 0:
            prm = (conv_w_in_b[li], conv_dw_w[li], conv_dw_b[li], conv_ln_g[li], conv_ln_b[li],
                   conv_w_out_b[li], norm_g[i], norm_b[i])
            zero_buf = jnp.zeros((bp, CONV_WIDTH - 1, E_CONV), F32)
            yp, yp_b, stp = _conv_layer(yp, zero_buf, *prm)
            ys, _, sts = _conv_layer(ys, state_conv[li], *prm)
            conv_p.append(stp)
            conv_s.append(sts)
        else:
            hs = _matmul(ys.reshape(bs * ts, D_MODEL), attn_w_in, li).reshape(bs, ts, -1)
            op, kv_p = _attn_prompt(yp_b, attn_w_pairs, li, kv_p)
            os_, kv_s = _attn_sample(hs, caches_t, li, kv_s)
            yp = _proj_ln(op.reshape(bp * tp, E_ATTN), yp.reshape(bp * tp, D_MODEL),
                          attn_w_out_b[li], norm_g[i], norm_b[i]).reshape(bp, tp, D_MODEL)
            ys = _proj_ln(os_.reshape(bs * ts, E_ATTN), ys.reshape(bs * ts, D_MODEL),
                          attn_w_out_b[li], norm_g[i], norm_b[i]).reshape(bs, ts, D_MODEL)
    return (yp, ys,
            _cache_time_major(kv_p[0]), _cache_time_major(kv_p[1]), _cache_time_major(kv_p[2]),
            jnp.stack(conv_p),
            _cache_time_major(kv_s[0]), _cache_time_major(kv_s[1]), _cache_time_major(kv_s[2]),
            jnp.stack(conv_s))
```

```python
import functools

import jax
import jax.numpy as jnp
import numpy as np
from jax import lax
from jax.experimental import pallas as pl
from jax.experimental.pallas import tpu as pltpu

D_MODEL = 1024
DEPTH = 4
E_CONV = D_MODEL
CONV_WIDTH = 31
HEAD_DIM = 64
N_HEADS = D_MODEL // HEAD_DIM
E_ATTN = N_HEADS * HEAD_DIM
GROUPS = ((128, 1), (512, 4), (2048, 16))
N_GROUPS = len(GROUPS)
ALPHA = (2.0 * DEPTH) ** 0.25
LN_EPS = 1e-5
NEG_INF = -1e30
QK_SCALE = HEAD_DIM ** -0.5

LANES = 128
SUBLANES = 8
HALO = 32
HALO_OFF = HALO - (CONV_WIDTH - 1)
CONV_COLS = 256
CONV_ROWS = 256
Q_BLK = 128
HEADS_PER_STEP = LANES // HEAD_DIM
SAMPLE_HEADS = 4
VMEM_LIMIT = 56 * 1024 * 1024

BF16 = jnp.bfloat16
F32 = jnp.float32


def _dot(a, b):
    return jnp.dot(a, b, preferred_element_type=F32)


def _dot_nt(a, b):
    return lax.dot_general(a, b, (((1,), (1,)), ((), ())), preferred_element_type=F32)


def _sigmoid(x):
    return 0.5 * jnp.tanh(0.5 * x) + 0.5


def _silu(x):
    h = 0.5 * x
    return h + h * jnp.tanh(h)


def _layer_norm(x, g, b):
    mu = jnp.mean(x, axis=-1, keepdims=True)
    xc = x - mu
    var = jnp.mean(xc * xc, axis=-1, keepdims=True)
    return xc * lax.rsqrt(var + LN_EPS) * g + b


def _merge_groups(outs, lses):
    mx = functools.reduce(jnp.maximum, lses)
    es = [jnp.exp(lv - mx) for lv in lses]
    den = functools.reduce(lambda a, b: a + b, es)
    num = functools.reduce(lambda a, b: a + b, [e * o for e, o in zip(es, outs)])
    return num / den


def _conv_layer_kernel(x_ref, buf_ref, win_ref, dww_ref, dwb_ref, lng_ref, lnb_ref, wout_ref,
                       ng_ref, nb_ref, o_ref, *rest, nb, tt, nt, rc, emit_bf16):
    ob_ref = rest[0] if emit_bf16 else None
    st_ref, ubuf, zbuf, cbuf, shbuf = rest[-5:]
    t = pl.program_id(1)
    n_rows = nb * tt

    @pl.when(t == 0)
    def _():
        ubuf[:, 0:HALO, :] = buf_ref[...]

    x = x_ref[...].reshape(n_rows, D_MODEL)
    xb = x.astype(BF16)

    n_sh = tt + HALO - SUBLANES
    for c0 in range(0, E_CONV, CONV_COLS):
        cols = slice(c0, c0 + CONV_COLS)
        a = _dot(xb, win_ref[:, c0:c0 + CONV_COLS])
        gl = _dot(xb, win_ref[:, E_CONV + c0:E_CONV + c0 + CONV_COLS])
        ubuf[:, HALO:HALO + tt, cols] = (a * _sigmoid(gl)).reshape(nb, tt, CONV_COLS)
        z = _dot(xb, win_ref[:, 2 * E_CONV + c0:2 * E_CONV + c0 + CONV_COLS])
        zbuf[:, cols] = _silu(z)
        for c in range(c0, c0 + CONV_COLS, LANES):
            lanes = slice(c, c + LANES)
            for s in range(1, SUBLANES):
                shbuf[s, :, :, :] = ubuf[:, s:s + n_sh, lanes]
            for r0 in range(0, tt, rc):
                acc = jnp.zeros((nb, rc, LANES), F32)
                for k in range(CONV_WIDTH):
                    lo = HALO_OFF + k + r0
                    s = lo % SUBLANES
                    base = lo - s
                    assert base + rc <= (n_sh if s else tt + HALO)
                    src = shbuf[s, :, base:base + rc, :] if s else ubuf[:, base:base + rc, lanes]
                    acc = acc + dww_ref[k:k + 1, lanes] * src
                cbuf[:, r0:r0 + rc, lanes] = acc

    cn = _layer_norm(cbuf[...].reshape(n_rows, E_CONV) + dwb_ref[...], lng_ref[...], lnb_ref[...])
    y = _silu(cn) * zbuf[...]
    f = _dot(y.astype(BF16), wout_ref[...])
    out = _layer_norm(ALPHA * x + f, ng_ref[...], nb_ref[...])
    o_ref[...] = out.reshape(nb, tt, D_MODEL)
    if emit_bf16:
        ob_ref[...] = out.astype(BF16).reshape(nb, tt, D_MODEL)

    @pl.when(t == nt - 1)
    def _():
        st_ref[...] = ubuf[:, tt:tt + HALO, :]

    if nt > 1:
        ubuf[:, 0:HALO, :] = ubuf[:, tt:tt + HALO, :]


def _conv_layer(x, buf, w_in, dw_w, dw_b, ln_g, ln_b, w_out, ng, nb_, emit_bf16):
    nbatch, seq, _ = x.shape
    tt = min(seq, CONV_ROWS)
    nb = max(1, min(nbatch, CONV_ROWS // seq))
    assert seq % tt == 0 and tt % SUBLANES == 0 and nbatch % nb == 0
    assert nb == 1 or not emit_bf16
    nt = seq // tt
    rc = min(tt, 32)
    buf_pad = jnp.pad(buf, ((0, 0), (HALO_OFF, 0), (0, 0)))
    row = lambda v: v.reshape(1, -1)
    const = lambda shape: pl.BlockSpec(shape, lambda b, t: (0,) * len(shape))
    tile = pl.BlockSpec((nb, tt, D_MODEL), lambda b, t: (b, t, 0))
    halo = pl.BlockSpec((nb, HALO, E_CONV), lambda b, t: (b, 0, 0))
    res = pl.pallas_call(
        functools.partial(_conv_layer_kernel, nb=nb, tt=tt, nt=nt, rc=rc, emit_bf16=emit_bf16),
        grid=(nbatch // nb, nt),
        in_specs=[
            tile, halo,
            const((D_MODEL, 3 * E_CONV)),
            const((CONV_WIDTH, E_CONV)),
            const((1, E_CONV)), const((1, E_CONV)), const((1, E_CONV)),
            const((E_CONV, D_MODEL)),
            const((1, D_MODEL)), const((1, D_MODEL)),
        ],
        out_specs=[tile] + ([tile] if emit_bf16 else []) + [halo],
        out_shape=[jax.ShapeDtypeStruct((nbatch, seq, D_MODEL), F32)]
        + ([jax.ShapeDtypeStruct((nbatch, seq, D_MODEL), BF16)] if emit_bf16 else [])
        + [jax.ShapeDtypeStruct((nbatch, HALO, E_CONV), F32)],
        scratch_shapes=[
            pltpu.VMEM((nb, HALO + tt, E_CONV), F32),
            pltpu.VMEM((nb * tt, E_CONV), F32),
            pltpu.VMEM((nb, tt, E_CONV), F32),
            pltpu.VMEM((SUBLANES, nb, tt + HALO - SUBLANES, LANES), F32),
        ],
        compiler_params=pltpu.CompilerParams(
            dimension_semantics=("arbitrary", "arbitrary"), vmem_limit_bytes=VMEM_LIMIT),
        name="conv_layer",
    )(x, buf_pad, w_in, dw_w, row(dw_b), row(ln_g), row(ln_b), w_out, row(ng), row(nb_))
    return res[0], (res[1] if emit_bf16 else None), res[-1][:, HALO_OFF:, :]


def _matmul_kernel(x_ref, w_ref, o_ref):
    o_ref[...] = _dot(x_ref[...].astype(BF16), w_ref[...].astype(BF16))


def _matmul(x, w, li):
    m, k = x.shape
    n = w.shape[2]
    tm = min(m, 512)
    tn = 1024
    assert m % tm == 0 and n % tn == 0
    return pl.pallas_call(
        _matmul_kernel,
        grid=(m // tm, n // tn),
        in_specs=[pl.BlockSpec((tm, k), lambda i, j: (i, 0)),
                  pl.BlockSpec((None, k, tn), lambda i, j: (li, 0, j))],
        out_specs=pl.BlockSpec((tm, tn), lambda i, j: (i, j)),
        out_shape=jax.ShapeDtypeStruct((m, n), F32),
        compiler_params=pltpu.CompilerParams(
            dimension_semantics=("arbitrary", "arbitrary"), vmem_limit_bytes=VMEM_LIMIT),
        name="attn_in_proj",
    )(x, w)


def _proj_ln_kernel(y_ref, x_ref, w_ref, g_ref, b_ref, o_ref):
    f = _dot(y_ref[...].astype(BF16), w_ref[...])
    o_ref[...] = _layer_norm(ALPHA * x_ref[...] + f, g_ref[...], b_ref[...])


def _proj_ln(y, x, w, g, b):
    m, k = y.shape
    n = w.shape[1]
    tm = min(m, 512)
    assert m % tm == 0
    return pl.pallas_call(
        _proj_ln_kernel,
        grid=(m // tm,),
        in_specs=[pl.BlockSpec((tm, k), lambda i: (i, 0)),
                  pl.BlockSpec((tm, n), lambda i: (i, 0)),
                  pl.BlockSpec((k, n), lambda i: (0, 0)),
                  pl.BlockSpec((1, n), lambda i: (0, 0)),
                  pl.BlockSpec((1, n), lambda i: (0, 0))],
        out_specs=pl.BlockSpec((tm, n), lambda i: (i, 0)),
        out_shape=jax.ShapeDtypeStruct((m, n), F32),
        compiler_params=pltpu.CompilerParams(
            dimension_semantics=("arbitrary",), vmem_limit_bytes=VMEM_LIMIT),
        name="attn_out_proj_ln",
    )(y, x, w, g.reshape(1, -1), b.reshape(1, -1))


N_COL_BLOCKS = 3 * N_GROUPS + 1
MM_ROWS = 512
MM_BLOCKS = 2
KV_T_COLS = 512
UNITS_PER_ITER = 8


def _attn_prompt_kernel(*refs, seq):
    n_scr = N_COL_BLOCKS + 3 * N_GROUPS
    x_ref = refs[0]
    w_refs = refs[1:1 + N_COL_BLOCKS]
    y_ref = refs[-(n_scr + N_GROUPS + 1)]
    kv_refs = refs[-(n_scr + N_GROUPS):-n_scr]
    h_scr = refs[-n_scr:-3 * N_GROUPS]
    o_scr = refs[-3 * N_GROUPS:-2 * N_GROUPS]
    m_scr = refs[-2 * N_GROUPS:-N_GROUPS]
    l_scr = refs[-N_GROUPS:]

    def project(i, carry):
        rs = pl.ds(pl.multiple_of(i * MM_ROWS, MM_ROWS), MM_ROWS)
        xb = x_ref[rs, :]
        for c in range(0, N_COL_BLOCKS, MM_BLOCKS):
            w = jnp.concatenate([w_refs[c + j][...] for j in range(MM_BLOCKS)], axis=1).astype(BF16)
            r = _dot(xb, w)
            for j in range(MM_BLOCKS):
                h_scr[c + j][rs, :] = r[:, j * LANES:(j + 1) * LANES]
        return carry

    lax.fori_loop(0, seq // MM_ROWS, project, 0)

    for g, (window, _) in enumerate(GROUPS):
        keep = min(window, seq)
        ch = min(keep, KV_T_COLS)
        for kv in range(2):
            src = h_scr[3 * g + 1 + kv]
            for c in range(keep // ch):
                lo = seq - keep + c * ch
                kv_refs[g][kv, :, c * ch:(c + 1) * ch] = src[lo:lo + ch, :].T

    n_stack = HEADS_PER_STEP * Q_BLK
    lane = lax.broadcasted_iota(jnp.int32, (Q_BLK, LANES), 1)
    first_head = lane < HEAD_DIM
    qi = lax.broadcasted_iota(jnp.int32, (n_stack, Q_BLK), 0) & (Q_BLK - 1)
    kj = lax.broadcasted_iota(jnp.int32, (n_stack, Q_BLK), 1)
    bias_cur = jnp.where(kj <= qi, 0.0, NEG_INF)
    bias_prev = jnp.where(kj >= qi, 0.0, NEG_INF)
    bias_both = jnp.concatenate([bias_prev, bias_cur], axis=1)
    key_row = lax.broadcasted_iota(jnp.int32, (n_stack, LANES), 0)
    key_lane = lax.broadcasted_iota(jnp.int32, (n_stack, LANES), 1)
    ones_all = jnp.ones((n_stack, LANES), BF16)
    ones_head = jnp.where((key_row < Q_BLK) == (key_lane < HEAD_DIM), 1.0, 0.0).astype(BF16)

    def rows(start, dil):
        if dil == 1:
            return pl.ds(start, Q_BLK)
        return pl.ds(start, Q_BLK, stride=dil)

    def unit_batch(g, dil, chains):
        q_ref, k_ref, v_ref = h_scr[3 * g: 3 * g + 3]
        for starts, lead_start in chains:
            if lead_start is None:
                k0 = k_ref[rows(starts[0], dil), :]
                v0 = v_ref[rows(starts[0], dil), :]
                k_bd = jnp.concatenate([jnp.where(first_head, k0, 0.0), jnp.where(first_head, 0.0, k0)],
                                       axis=0).astype(BF16)
                v_bd = jnp.concatenate([jnp.where(first_head, v0, 0.0), jnp.where(first_head, 0.0, v0)],
                                       axis=0).astype(BF16)
                v_bd = jnp.concatenate([v_bd, ones_head], axis=1)
                ks, vs = [k0.astype(BF16)], [v0.astype(BF16)]
            else:
                ks, vs = [], []
                k_lead = k_ref[rows(lead_start, dil), :].astype(BF16)
                v_lead = v_ref[rows(lead_start, dil), :].astype(BF16)
            ks += [k_ref[rows(st, dil), :].astype(BF16) for st in starts[len(ks):]]
            vs += [v_ref[rows(st, dil), :].astype(BF16) for st in starts[len(vs):]]
            for a, st in enumerate(starts):
                q = q_ref[rows(st, dil), :] * QK_SCALE
                if a > 0 or lead_start is not None:
                    qs = jnp.concatenate([jnp.where(first_head, q, 0.0), jnp.where(first_head, 0.0, q)],
                                         axis=0).astype(BF16)
                    kp, vp = (ks[a - 1], vs[a - 1]) if a > 0 else (k_lead, v_lead)
                    s = _dot_nt(qs, jnp.concatenate([kp, ks[a]], axis=0)) + bias_both
                    m = jnp.max(s, axis=-1, keepdims=True)
                    p = jnp.exp(s - m)
                    vv = jnp.concatenate([jnp.concatenate([vp, vs[a]], axis=0), ones_all], axis=1)
                    ol = _dot(p.astype(BF16), vv)
                    o = jnp.where(first_head, ol[0:Q_BLK, 0:LANES], ol[Q_BLK:, 0:LANES])
                    l = jnp.where(first_head, ol[0:Q_BLK, LANES:], ol[Q_BLK:, LANES:])
                    m0, m1 = m[0:Q_BLK], m[Q_BLK:]
                else:
                    s = _dot_nt(q.astype(BF16), k_bd)
                    s0 = s[:, 0:Q_BLK] + bias_cur[0:Q_BLK]
                    s1 = s[:, Q_BLK:] + bias_cur[0:Q_BLK]
                    m0 = jnp.max(s0, axis=-1, keepdims=True)
                    m1 = jnp.max(s1, axis=-1, keepdims=True)
                    p0 = jnp.exp(s0 - m0)
                    p1 = jnp.exp(s1 - m1)
                    ol = _dot(jnp.concatenate([p0, p1], axis=1).astype(BF16), v_bd)
                    o, l = ol[:, 0:LANES], ol[:, LANES:]
                o_scr[g][rows(st, dil), :] = o
                m_scr[g][rows(st, dil), :] = jnp.where(first_head, m0, m1)
                l_scr[g][rows(st, dil), :] = l

    for g, (_, dil) in enumerate(GROUPS):
        n_blk = seq // (dil * Q_BLK)
        step = dil * Q_BLK
        if n_blk <= UNITS_PER_ITER:
            n_cls = UNITS_PER_ITER // n_blk
            assert dil % n_cls == 0

            def classes(i, carry, g=g, dil=dil, n_blk=n_blk, step=step, n_cls=n_cls):
                unit_batch(g, dil, [([i * n_cls + c + a * step for a in range(n_blk)], None)
                                    for c in range(n_cls)])
                return carry

            lax.fori_loop(0, dil // n_cls, classes, 0)
        else:
            assert n_blk % UNITS_PER_ITER == 0

            def per_class(r, carry, g=g, dil=dil, n_blk=n_blk, step=step):
                unit_batch(g, dil, [([r + a * step for a in range(UNITS_PER_ITER)], None)])

                def later(i, c2):
                    s0 = r + i * (UNITS_PER_ITER * step)
                    unit_batch(g, dil, [([s0 + a * step for a in range(UNITS_PER_ITER)], s0 - step)])
                    return c2

                lax.fori_loop(1, n_blk // UNITS_PER_ITER, later, 0)
                return carry

            lax.fori_loop(0, dil, per_class, 0)

    chunk = 256
    z_scr = h_scr[N_COL_BLOCKS - 1]

    def merge(i, carry):
        rs = pl.ds(pl.multiple_of(i * chunk, chunk), chunk)
        ms = [m_scr[g][rs, :] for g in range(N_GROUPS)]
        mx = functools.reduce(jnp.maximum, ms)
        es = [jnp.exp(mv - mx) for mv in ms]
        num = functools.reduce(lambda a, b: a + b, [es[g] * o_scr[g][rs, :] for g in range(N_GROUPS)])
        den = functools.reduce(lambda a, b: a + b, [es[g] * l_scr[g][rs, :] for g in range(N_GROUPS)])
        z = z_scr[rs, :]
        y_ref[rs, :] = ((num / den) * _silu(z)).astype(BF16)
        return carry

    lax.fori_loop(0, seq // chunk, merge, 0)


def _attn_prompt(xb, w_in, li, prevs):
    nbatch, seq, _ = xb.shape
    n_layers = w_in.shape[0]
    n_pair = E_ATTN // LANES
    assert seq % (GROUPS[-1][1] * Q_BLK) == 0 and seq % MM_ROWS == 0 and N_COL_BLOCKS % MM_BLOCKS == 0
    keeps = [min(window, seq) for window, _ in GROUPS]

    def kv_spec(keep):
        return pl.BlockSpec((None, None, 2, LANES, keep), lambda b, hp: (li, b, 0, hp, 0))

    in_specs = [pl.BlockSpec((None, seq, D_MODEL), lambda b, hp: (b, 0, 0))]
    in_specs += [pl.BlockSpec((None, D_MODEL, LANES), lambda b, hp, blk=blk: (li, 0, blk * n_pair + hp))
                 for blk in range(N_COL_BLOCKS)]
    args = [xb] + [w_in] * N_COL_BLOCKS
    aliases = {}
    if prevs is not None:
        for g in range(N_GROUPS):
            aliases[len(args)] = 1 + g
            in_specs.append(pl.BlockSpec(memory_space=pl.ANY))
            args.append(prevs[g])
    res = pl.pallas_call(
        functools.partial(_attn_prompt_kernel, seq=seq),
        grid=(nbatch, n_pair),
        in_specs=in_specs,
        out_specs=[pl.BlockSpec((None, seq, LANES), lambda b, hp: (b, 0, hp))] + [kv_spec(k) for k in keeps],
        out_shape=[jax.ShapeDtypeStruct((nbatch, seq, E_ATTN), BF16)]
        + [jax.ShapeDtypeStruct((n_layers, nbatch, 2, E_ATTN, k), F32) for k in keeps],
        scratch_shapes=[pltpu.VMEM((seq, LANES), F32) for _ in range(N_COL_BLOCKS + 3 * N_GROUPS)],
        input_output_aliases=aliases,
        compiler_params=pltpu.CompilerParams(
            dimension_semantics=("arbitrary", "arbitrary"), vmem_limit_bytes=VMEM_LIMIT),
        name="attn_prompt",
    )(*args)
    return res[0], list(res[1:])


def _sample_biases(window, dil, t_new, n_heads):
    qi = np.arange(n_heads * t_new)[:, None] % t_new
    old_off = np.arange(window)[None, :] - window
    new_off = np.arange(LANES)[None, :] - (LANES - t_new)
    res = []
    for off, slot_ok in ((old_off, old_off < 0), (new_off, new_off >= 0)):
        dist = qi - off
        ok = slot_ok & (dist >= 0) & (dist <= window) & (dist % dil == 0)
        res.append(np.where(ok, 0.0, NEG_INF).astype(np.float32))
    return res


def _attn_sample_kernel(*refs, t_new):
    n_in = 10 + 3 * N_GROUPS
    h_refs = refs[:10]
    c_refs = refs[10:10 + N_GROUPS]
    b_refs = refs[10 + N_GROUPS:n_in]
    y_ref = refs[-(N_GROUPS + 1)]
    o_refs = refs[-N_GROUPS:]

    width = SAMPLE_HEADS * HEAD_DIM
    n_rows = SAMPLE_HEADS * t_new
    row = lax.broadcasted_iota(jnp.int32, (n_rows, width), 0)
    col = lax.broadcasted_iota(jnp.int32, (n_rows, width), 1)
    diag = jnp.right_shift(row, t_new.bit_length() - 1) == jnp.right_shift(col, HEAD_DIM.bit_length() - 1)
    new_lanes = lax.broadcasted_iota(jnp.int32, (width, LANES), 1) >= LANES - t_new
    zpad = jnp.zeros((LANES - t_new, width), F32)

    outs, lses = [], []
    for g, (window, _) in enumerate(GROUPS):
        q = h_refs[3 * g][...] * QK_SCALE
        kt_new = jnp.concatenate([zpad, h_refs[3 * g + 1][...]], axis=0).T
        vt_new = jnp.concatenate([zpad, h_refs[3 * g + 2][...]], axis=0).T
        kt = c_refs[g][0]
        vt = c_refs[g][1]
        for idx, (old, new) in enumerate(((kt, kt_new), (vt, vt_new))):
            rolled = pltpu.roll(old, window - t_new, axis=1)
            if window > LANES:
                o_refs[g][idx, :, 0:window - LANES] = rolled[:, 0:window - LANES]
            o_refs[g][idx, :, window - LANES:window] = jnp.where(new_lanes, new, rolled[:, window - LANES:])

        q_rep = jnp.broadcast_to(q[None], (SAMPLE_HEADS, t_new, width)).reshape(n_rows, width)
        q_bd = jnp.where(diag, q_rep, 0.0).astype(BF16)
        s_old = _dot(q_bd, kt.astype(BF16)) + b_refs[2 * g][...]
        s_new = _dot(q_bd, kt_new.astype(BF16)) + b_refs[2 * g + 1][...]
        m = jnp.maximum(jnp.max(s_old, axis=-1, keepdims=True), jnp.max(s_new, axis=-1, keepdims=True))
        p_old = jnp.exp(s_old - m)
        p_new = jnp.exp(s_new - m)
        l = jnp.sum(p_old, axis=-1, keepdims=True) + jnp.sum(p_new, axis=-1, keepdims=True)
        o = (_dot_nt(p_old.astype(BF16), vt.astype(BF16))
             + _dot_nt(p_new.astype(BF16), vt_new.astype(BF16))) * (1.0 / l)
        lse = m + jnp.log(l)
        outs.append(jnp.where(diag, o, 0.0).reshape(SAMPLE_HEADS, t_new, width).sum(axis=0))
        lses.append(jnp.where(diag, lse, 0.0).reshape(SAMPLE_HEADS, t_new, width).sum(axis=0))

    z = h_refs[9][...]
    y_ref[...] = _merge_groups(outs, lses) * _silu(z)


def _attn_sample(h, caches_t, li, prevs):
    nbatch, t_new, _ = h.shape
    n_layers = caches_t[0].shape[0]
    width = SAMPLE_HEADS * HEAD_DIM
    n_chunk = E_ATTN // width
    assert t_new & (t_new - 1) == 0 and t_new <= SUBLANES

    def h_spec(blk):
        return pl.BlockSpec((None, t_new, width), lambda b, c: (b, 0, blk * n_chunk + c))

    def cache_spec(window):
        return pl.BlockSpec((None, None, 2, width, window), lambda b, c: (li, b, 0, c, 0))

    biases = []
    for window, dil in GROUPS:
        biases += [jnp.asarray(a) for a in _sample_biases(window, dil, t_new, SAMPLE_HEADS)]
    full = lambda a: pl.BlockSpec(a.shape, lambda b, c: (0, 0))

    in_specs = [h_spec(blk) for blk in range(10)]
    in_specs += [cache_spec(window) for window, _ in GROUPS]
    in_specs += [full(a) for a in biases]
    args = [h] * 10 + list(caches_t) + biases
    aliases = {}
    if prevs is not None:
        for g in range(N_GROUPS):
            aliases[len(args)] = 1 + g
            in_specs.append(pl.BlockSpec(memory_space=pl.ANY))
            args.append(prevs[g])
    res = pl.pallas_call(
        functools.partial(_attn_sample_kernel, t_new=t_new),
        grid=(nbatch, n_chunk),
        in_specs=in_specs,
        out_specs=[pl.BlockSpec((None, t_new, width), lambda b, c: (b, 0, c))]
        + [cache_spec(window) for window, _ in GROUPS],
        out_shape=[jax.ShapeDtypeStruct((nbatch, t_new, E_ATTN), F32)]
        + [jax.ShapeDtypeStruct((n_layers, nbatch, 2, E_ATTN, window), F32) for window, _ in GROUPS],
        input_output_aliases=aliases,
        compiler_params=pltpu.CompilerParams(
            dimension_semantics=("arbitrary", "arbitrary"), vmem_limit_bytes=VMEM_LIMIT),
        name="attn_sample",
    )(*args)
    return res[0], list(res[1:])


def _cache_time_minor(c):
    n_layers, nbatch, window = c.shape[:3]
    return jnp.transpose(c, (0, 1, 3, 4, 5, 2)).reshape(n_layers, nbatch, 2, E_ATTN, window)


def _cache_time_major(c):
    n_layers, nbatch, window = c.shape[0], c.shape[1], c.shape[-1]
    c = c.reshape(n_layers, nbatch, 2, N_HEADS, HEAD_DIM, window)
    return jnp.transpose(c, (0, 1, 5, 2, 3, 4))


def kernel(x_prompt, x_sample, cache_kv_w128_d1, cache_kv_w512_d4, cache_kv_w2048_d16, state_conv,
           conv_w_in, conv_dw_w, conv_dw_b, conv_ln_g, conv_ln_b, conv_w_out,
           attn_w_in, attn_w_out, norm_g, norm_b):
    caches_t = [_cache_time_minor(c) for c in (cache_kv_w128_d1, cache_kv_w512_d4, cache_kv_w2048_d16)]
    conv_w_in_b = conv_w_in.astype(BF16)
    conv_w_out_b = conv_w_out.astype(BF16)
    attn_w_out_b = attn_w_out.astype(BF16)

    yp, ys = x_prompt, x_sample
    bp, tp, _ = yp.shape
    bs, ts, _ = ys.shape
    kv_p = None
    kv_s = None
    conv_p, conv_s = [], []
    for i in range(DEPTH):
        li = i // 2
        if i % 2 == 0:
            prm = (conv_w_in_b[li], conv_dw_w[li], conv_dw_b[li], conv_ln_g[li], conv_ln_b[li],
                   conv_w_out_b[li], norm_g[i], norm_b[i])
            zero_buf = jnp.zeros((bp, CONV_WIDTH - 1, E_CONV), F32)
            yp, yp_b, stp = _conv_layer(yp, zero_buf, *prm, emit_bf16=True)
            ys, _, sts = _conv_layer(ys, state_conv[li], *prm, emit_bf16=False)
            conv_p.append(stp)
            conv_s.append(sts)
        else:
            hs = _matmul(ys.reshape(bs * ts, D_MODEL), attn_w_in, li).reshape(bs, ts, -1)
            op, kv_p = _attn_prompt(yp_b, attn_w_in, li, kv_p)
            os_, kv_s = _attn_sample(hs, caches_t, li, kv_s)
            yp = _proj_ln(op.reshape(bp * tp, E_ATTN), yp.reshape(bp * tp, D_MODEL),
                          attn_w_out_b[li], norm_g[i], norm_b[i]).reshape(bp, tp, D_MODEL)
            ys = _proj_ln(os_.reshape(bs * ts, E_ATTN), ys.reshape(bs * ts, D_MODEL),
                          attn_w_out_b[li], norm_g[i], norm_b[i]).reshape(bs, ts, D_MODEL)
    return (yp, ys,
            _cache_time_major(kv_p[0]), _cache_time_major(kv_p[1]), _cache_time_major(kv_p[2]),
            jnp.stack(conv_p),
            _cache_time_major(kv_s[0]), _cache_time_major(kv_s[1]), _cache_time_major(kv_s[2]),
            jnp.stack(conv_s))
```

```python
import functools

import jax
import jax.numpy as jnp
import numpy as np
from jax import lax
from jax.experimental import pallas as pl
from jax.experimental.pallas import tpu as pltpu

D_MODEL = 1024
DEPTH = 4
E_CONV = D_MODEL
CONV_WIDTH = 31
HEAD_DIM = 64
N_HEADS = D_MODEL // HEAD_DIM
E_ATTN = N_HEADS * HEAD_DIM
GROUPS = ((128, 1), (512, 4), (2048, 16))
N_GROUPS = len(GROUPS)
ALPHA = (2.0 * DEPTH) ** 0.25
LN_EPS = 1e-5
NEG_INF = -1e30
QK_SCALE = HEAD_DIM ** -0.5

LANES = 128
SUBLANES = 8
HALO = 32
HALO_OFF = HALO - (CONV_WIDTH - 1)
CONV_COLS = 256
CONV_ROWS = 256
Q_BLK = 128
HEADS_PER_STEP = LANES // HEAD_DIM
SAMPLE_HEADS = 4
VMEM_LIMIT = 56 * 1024 * 1024

BF16 = jnp.bfloat16
F32 = jnp.float32


def _dot(a, b):
    return jnp.dot(a, b, preferred_element_type=F32)


def _dot_nt(a, b):
    return lax.dot_general(a, b, (((1,), (1,)), ((), ())), preferred_element_type=F32)


def _sigmoid(x):
    return 0.5 * jnp.tanh(0.5 * x) + 0.5


def _silu(x):
    h = 0.5 * x
    return h + h * jnp.tanh(h)


def _layer_norm(x, g, b):
    mu = jnp.mean(x, axis=-1, keepdims=True)
    xc = x - mu
    var = jnp.mean(xc * xc, axis=-1, keepdims=True)
    return xc * lax.rsqrt(var + LN_EPS) * g + b


def _merge_groups(outs, lses):
    mx = functools.reduce(jnp.maximum, lses)
    es = [jnp.exp(lv - mx) for lv in lses]
    den = functools.reduce(lambda a, b: a + b, es)
    num = functools.reduce(lambda a, b: a + b, [e * o for e, o in zip(es, outs)])
    return num / den


def _conv_layer_kernel(x_ref, buf_ref, win_ref, dww_ref, dwb_ref, lng_ref, lnb_ref, wout_ref,
                       ng_ref, nb_ref, o_ref, *rest, nb, tt, nt, rc, emit_bf16):
    ob_ref = rest[0] if emit_bf16 else None
    st_ref, ubuf, zbuf, cbuf, shbuf = rest[-5:]
    t = pl.program_id(1)
    n_rows = nb * tt

    @pl.when(t == 0)
    def _():
        ubuf[:, 0:HALO, :] = buf_ref[...]

    x = x_ref[...].reshape(n_rows, D_MODEL)
    xb = x.astype(BF16)

    n_sh = tt + HALO - SUBLANES
    for c0 in range(0, E_CONV, CONV_COLS):
        cols = slice(c0, c0 + CONV_COLS)
        a = _dot(xb, win_ref[:, c0:c0 + CONV_COLS])
        gl = _dot(xb, win_ref[:, E_CONV + c0:E_CONV + c0 + CONV_COLS])
        ubuf[:, HALO:HALO + tt, cols] = (a * _sigmoid(gl)).reshape(nb, tt, CONV_COLS)
        z = _dot(xb, win_ref[:, 2 * E_CONV + c0:2 * E_CONV + c0 + CONV_COLS])
        zbuf[:, cols] = _silu(z)
        for c in range(c0, c0 + CONV_COLS, LANES):
            lanes = slice(c, c + LANES)
            for s in range(1, SUBLANES):
                shbuf[s, :, :, :] = ubuf[:, s:s + n_sh, lanes]
            for r0 in range(0, tt, rc):
                acc = jnp.zeros((nb, rc, LANES), F32)
                for k in range(CONV_WIDTH):
                    lo = HALO_OFF + k + r0
                    s = lo % SUBLANES
                    base = lo - s
                    assert base + rc <= (n_sh if s else tt + HALO)
                    src = shbuf[s, :, base:base + rc, :] if s else ubuf[:, base:base + rc, lanes]
                    acc = acc + dww_ref[k:k + 1, lanes] * src
                cbuf[:, r0:r0 + rc, lanes] = acc

    cn = _layer_norm(cbuf[...].reshape(n_rows, E_CONV) + dwb_ref[...], lng_ref[...], lnb_ref[...])
    y = _silu(cn) * zbuf[...]
    f = _dot(y.astype(BF16), wout_ref[...])
    out = _layer_norm(ALPHA * x + f, ng_ref[...], nb_ref[...])
    o_ref[...] = out.reshape(nb, tt, D_MODEL)
    if emit_bf16:
        ob_ref[...] = out.astype(BF16).reshape(nb, tt, D_MODEL)

    @pl.when(t == nt - 1)
    def _():
        st_ref[...] = ubuf[:, tt:tt + HALO, :]

    if nt > 1:
        ubuf[:, 0:HALO, :] = ubuf[:, tt:tt + HALO, :]


def _conv_layer(x, buf, w_in, dw_w, dw_b, ln_g, ln_b, w_out, ng, nb_, emit_bf16):
    nbatch, seq, _ = x.shape
    tt = min(seq, CONV_ROWS)
    nb = max(1, min(nbatch, CONV_ROWS // seq))
    assert seq % tt == 0 and tt % SUBLANES == 0 and nbatch % nb == 0
    assert nb == 1 or not emit_bf16
    nt = seq // tt
    rc = min(tt, 32)
    buf_pad = jnp.pad(buf, ((0, 0), (HALO_OFF, 0), (0, 0)))
    row = lambda v: v.reshape(1, -1)
    const = lambda shape: pl.BlockSpec(shape, lambda b, t: (0,) * len(shape))
    tile = pl.BlockSpec((nb, tt, D_MODEL), lambda b, t: (b, t, 0))
    halo = pl.BlockSpec((nb, HALO, E_CONV), lambda b, t: (b, 0, 0))
    res = pl.pallas_call(
        functools.partial(_conv_layer_kernel, nb=nb, tt=tt, nt=nt, rc=rc, emit_bf16=emit_bf16),
        grid=(nbatch // nb, nt),
        in_specs=[
            tile, halo,
            const((D_MODEL, 3 * E_CONV)),
            const((CONV_WIDTH, E_CONV)),
            const((1, E_CONV)), const((1, E_CONV)), const((1, E_CONV)),
            const((E_CONV, D_MODEL)),
            const((1, D_MODEL)), const((1, D_MODEL)),
        ],
        out_specs=[tile] + ([tile] if emit_bf16 else []) + [halo],
        out_shape=[jax.ShapeDtypeStruct((nbatch, seq, D_MODEL), F32)]
        + ([jax.ShapeDtypeStruct((nbatch, seq, D_MODEL), BF16)] if emit_bf16 else [])
        + [jax.ShapeDtypeStruct((nbatch, HALO, E_CONV), F32)],
        scratch_shapes=[
            pltpu.VMEM((nb, HALO + tt, E_CONV), F32),
            pltpu.VMEM((nb * tt, E_CONV), F32),
            pltpu.VMEM((nb, tt, E_CONV), F32),
            pltpu.VMEM((SUBLANES, nb, tt + HALO - SUBLANES, LANES), F32),
        ],
        compiler_params=pltpu.CompilerParams(
            dimension_semantics=("arbitrary", "arbitrary"), vmem_limit_bytes=VMEM_LIMIT),
        name="conv_layer",
    )(x, buf_pad, w_in, dw_w, row(dw_b), row(ln_g), row(ln_b), w_out, row(ng), row(nb_))
    return res[0], (res[1] if emit_bf16 else None), res[-1][:, HALO_OFF:, :]


def _matmul_kernel(x_ref, w_ref, o_ref):
    o_ref[...] = _dot(x_ref[...].astype(BF16), w_ref[...].astype(BF16))


def _matmul(x, w, li):
    m, k = x.shape
    n = w.shape[2]
    tm = min(m, 512)
    tn = 1024
    assert m % tm == 0 and n % tn == 0
    return pl.pallas_call(
        _matmul_kernel,
        grid=(m // tm, n // tn),
        in_specs=[pl.BlockSpec((tm, k), lambda i, j: (i, 0)),
                  pl.BlockSpec((None, k, tn), lambda i, j: (li, 0, j))],
        out_specs=pl.BlockSpec((tm, tn), lambda i, j: (i, j)),
        out_shape=jax.ShapeDtypeStruct((m, n), F32),
        compiler_params=pltpu.CompilerParams(
            dimension_semantics=("arbitrary", "arbitrary"), vmem_limit_bytes=VMEM_LIMIT),
        name="attn_in_proj",
    )(x, w)


def _proj_ln_kernel(y_ref, x_ref, w_ref, g_ref, b_ref, o_ref):
    f = _dot(y_ref[...].astype(BF16), w_ref[...])
    o_ref[...] = _layer_norm(ALPHA * x_ref[...] + f, g_ref[...], b_ref[...])


def _proj_ln(y, x, w, g, b):
    m, k = y.shape
    n = w.shape[1]
    tm = min(m, 512)
    assert m % tm == 0
    return pl.pallas_call(
        _proj_ln_kernel,
        grid=(m // tm,),
        in_specs=[pl.BlockSpec((tm, k), lambda i: (i, 0)),
                  pl.BlockSpec((tm, n), lambda i: (i, 0)),
                  pl.BlockSpec((k, n), lambda i: (0, 0)),
                  pl.BlockSpec((1, n), lambda i: (0, 0)),
                  pl.BlockSpec((1, n), lambda i: (0, 0))],
        out_specs=pl.BlockSpec((tm, n), lambda i: (i, 0)),
        out_shape=jax.ShapeDtypeStruct((m, n), F32),
        compiler_params=pltpu.CompilerParams(
            dimension_semantics=("arbitrary",), vmem_limit_bytes=VMEM_LIMIT),
        name="attn_out_proj_ln",
    )(y, x, w, g.reshape(1, -1), b.reshape(1, -1))


N_COL_BLOCKS = 3 * N_GROUPS + 1
MM_ROWS = 512
MM_BLOCKS = 2
KV_T_COLS = 512
GROUP_PITCH = (1, 4, 24)
UNITS_PER_ITER = 8


def _attn_prompt_kernel(*refs, seq):
    n_scr = N_COL_BLOCKS + 3 * N_GROUPS
    x_ref = refs[0]
    w_refs = refs[1:1 + N_COL_BLOCKS]
    y_ref = refs[-(n_scr + N_GROUPS + 1)]
    kv_refs = refs[-(n_scr + N_GROUPS):-n_scr]
    h_scr = refs[-n_scr:-3 * N_GROUPS]
    o_scr = refs[-3 * N_GROUPS:-2 * N_GROUPS]
    m_scr = refs[-2 * N_GROUPS:-N_GROUPS]
    l_scr = refs[-N_GROUPS:]

    def token_rows(g, t0, n):
        if g >= N_GROUPS or GROUP_PITCH[g] == GROUPS[g][1]:
            return [(t0, n)]
        dil, pitch = GROUPS[g][1], GROUP_PITCH[g]
        base = t0 // dil * pitch
        if not isinstance(base, int):
            base = pl.multiple_of(base, SUBLANES)
        return [(base + jj * pitch, dil) for jj in range(n // dil)]

    def load_tokens(ref, g, t0, n):
        return jnp.concatenate([ref[pl.ds(st, sz), :] for st, sz in token_rows(g, t0, n)], axis=0)

    def store_tokens(ref, g, t0, value):
        lo = 0
        for st, sz in token_rows(g, t0, value.shape[0]):
            ref[pl.ds(st, sz), :] = value[lo:lo + sz]
            lo += sz

    def project(i, carry):
        t0 = pl.multiple_of(i * MM_ROWS, MM_ROWS)
        xb = x_ref[pl.ds(t0, MM_ROWS), :]
        for c in range(0, N_COL_BLOCKS, MM_BLOCKS):
            w = jnp.concatenate([w_refs[c + j][...] for j in range(MM_BLOCKS)], axis=1).astype(BF16)
            r = _dot(xb, w)
            for j in range(MM_BLOCKS):
                store_tokens(h_scr[c + j], (c + j) // 3, t0, r[:, j * LANES:(j + 1) * LANES])
        return carry

    lax.fori_loop(0, seq // MM_ROWS, project, 0)

    for g, (window, _) in enumerate(GROUPS):
        keep = min(window, seq)
        ch = min(keep, KV_T_COLS)
        for kv in range(2):
            src = h_scr[3 * g + 1 + kv]
            for c in range(keep // ch):
                lo = seq - keep + c * ch
                kv_refs[g][kv, :, c * ch:(c + 1) * ch] = load_tokens(src, g, lo, ch).T

    n_stack = HEADS_PER_STEP * Q_BLK
    lane = lax.broadcasted_iota(jnp.int32, (Q_BLK, LANES), 1)
    first_head = lane < HEAD_DIM
    qi = lax.broadcasted_iota(jnp.int32, (n_stack, Q_BLK), 0) & (Q_BLK - 1)
    kj = lax.broadcasted_iota(jnp.int32, (n_stack, Q_BLK), 1)
    bias_cur = jnp.where(kj <= qi, 0.0, NEG_INF)
    bias_prev = jnp.where(kj >= qi, 0.0, NEG_INF)
    bias_both = jnp.concatenate([bias_prev, bias_cur], axis=1)
    key_row = lax.broadcasted_iota(jnp.int32, (n_stack, LANES), 0)
    key_lane = lax.broadcasted_iota(jnp.int32, (n_stack, LANES), 1)
    ones_all = jnp.ones((n_stack, LANES), BF16)
    ones_head = jnp.where((key_row < Q_BLK) == (key_lane < HEAD_DIM), 1.0, 0.0).astype(BF16)

    def rows(start, g):
        if GROUP_PITCH[g] == 1:
            return pl.ds(start, Q_BLK)
        return pl.ds(start, Q_BLK, stride=GROUP_PITCH[g])

    def unit_batch(g, dil, chains):
        q_ref, k_ref, v_ref = h_scr[3 * g: 3 * g + 3]
        for starts, lead_start in chains:
            if lead_start is None:
                k0 = k_ref[rows(starts[0], g), :]
                v0 = v_ref[rows(starts[0], g), :]
                k_bd = jnp.concatenate([jnp.where(first_head, k0, 0.0), jnp.where(first_head, 0.0, k0)],
                                       axis=0).astype(BF16)
                v_bd = jnp.concatenate([jnp.where(first_head, v0, 0.0), jnp.where(first_head, 0.0, v0)],
                                       axis=0).astype(BF16)
                v_bd = jnp.concatenate([v_bd, ones_head], axis=1)
                ks, vs = [k0.astype(BF16)], [v0.astype(BF16)]
            else:
                ks, vs = [], []
                k_lead = k_ref[rows(lead_start, g), :].astype(BF16)
                v_lead = v_ref[rows(lead_start, g), :].astype(BF16)
            ks += [k_ref[rows(st, g), :].astype(BF16) for st in starts[len(ks):]]
            vs += [v_ref[rows(st, g), :].astype(BF16) for st in starts[len(vs):]]
            for a, st in enumerate(starts):
                q = q_ref[rows(st, g), :] * QK_SCALE
                if a > 0 or lead_start is not None:
                    qs = jnp.concatenate([jnp.where(first_head, q, 0.0), jnp.where(first_head, 0.0, q)],
                                         axis=0).astype(BF16)
                    kp, vp = (ks[a - 1], vs[a - 1]) if a > 0 else (k_lead, v_lead)
                    s = _dot_nt(qs, jnp.concatenate([kp, ks[a]], axis=0)) + bias_both
                    m = jnp.max(s, axis=-1, keepdims=True)
                    p = jnp.exp(s - m)
                    vv = jnp.concatenate([jnp.concatenate([vp, vs[a]], axis=0), ones_all], axis=1)
                    ol = _dot(p.astype(BF16), vv)
                    o = jnp.where(first_head, ol[0:Q_BLK, 0:LANES], ol[Q_BLK:, 0:LANES])
                    l = jnp.where(first_head, ol[0:Q_BLK, LANES:], ol[Q_BLK:, LANES:])
                    m0, m1 = m[0:Q_BLK], m[Q_BLK:]
                else:
                    s = _dot_nt(q.astype(BF16), k_bd)
                    s0 = s[:, 0:Q_BLK] + bias_cur[0:Q_BLK]
                    s1 = s[:, Q_BLK:] + bias_cur[0:Q_BLK]
                    m0 = jnp.max(s0, axis=-1, keepdims=True)
                    m1 = jnp.max(s1, axis=-1, keepdims=True)
                    p0 = jnp.exp(s0 - m0)
                    p1 = jnp.exp(s1 - m1)
                    ol = _dot(jnp.concatenate([p0, p1], axis=1).astype(BF16), v_bd)
                    o, l = ol[:, 0:LANES], ol[:, LANES:]
                o_scr[g][rows(st, g), :] = o
                m_scr[g][rows(st, g), :] = jnp.where(first_head, m0, m1)
                l_scr[g][rows(st, g), :] = l

    for g, (_, dil) in enumerate(GROUPS):
        n_blk = seq // (dil * Q_BLK)
        step = GROUP_PITCH[g] * Q_BLK
        if n_blk <= UNITS_PER_ITER:
            n_cls = UNITS_PER_ITER // n_blk
            assert dil % n_cls == 0

            def classes(i, carry, g=g, dil=dil, n_blk=n_blk, step=step, n_cls=n_cls):
                unit_batch(g, dil, [([i * n_cls + c + a * step for a in range(n_blk)], None)
                                    for c in range(n_cls)])
                return carry

            lax.fori_loop(0, dil // n_cls, classes, 0)
        else:
            assert n_blk % UNITS_PER_ITER == 0

            def per_class(r, carry, g=g, dil=dil, n_blk=n_blk, step=step):
                unit_batch(g, dil, [([r + a * step for a in range(UNITS_PER_ITER)], None)])

                def later(i, c2):
                    s0 = r + i * (UNITS_PER_ITER * step)
                    unit_batch(g, dil, [([s0 + a * step for a in range(UNITS_PER_ITER)], s0 - step)])
                    return c2

                lax.fori_loop(1, n_blk // UNITS_PER_ITER, later, 0)
                return carry

            lax.fori_loop(0, dil, per_class, 0)

    chunk = 256
    z_scr = h_scr[N_COL_BLOCKS - 1]

    def merge(i, carry):
        t0 = pl.multiple_of(i * chunk, chunk)
        ms = [load_tokens(m_scr[g], g, t0, chunk) for g in range(N_GROUPS)]
        mx = functools.reduce(jnp.maximum, ms)
        es = [jnp.exp(mv - mx) for mv in ms]
        num = functools.reduce(lambda a, b: a + b,
                               [es[g] * load_tokens(o_scr[g], g, t0, chunk) for g in range(N_GROUPS)])
        den = functools.reduce(lambda a, b: a + b,
                               [es[g] * load_tokens(l_scr[g], g, t0, chunk) for g in range(N_GROUPS)])
        z = z_scr[pl.ds(t0, chunk), :]
        y_ref[pl.ds(t0, chunk), :] = ((num / den) * _silu(z)).astype(BF16)
        return carry

    lax.fori_loop(0, seq // chunk, merge, 0)


def _attn_prompt(xb, w_in, li, prevs):
    nbatch, seq, _ = xb.shape
    n_layers = w_in.shape[0]
    n_pair = E_ATTN // LANES
    assert seq % (GROUPS[-1][1] * Q_BLK) == 0 and seq % MM_ROWS == 0 and N_COL_BLOCKS % MM_BLOCKS == 0
    keeps = [min(window, seq) for window, _ in GROUPS]
    scr_rows = [seq // dil * pitch for (_, dil), pitch in zip(GROUPS, GROUP_PITCH)]

    def kv_spec(keep):
        return pl.BlockSpec((None, None, 2, LANES, keep), lambda b, hp: (li, b, 0, hp, 0))

    in_specs = [pl.BlockSpec((None, seq, D_MODEL), lambda b, hp: (b, 0, 0))]
    in_specs += [pl.BlockSpec((None, D_MODEL, LANES), lambda b, hp, blk=blk: (li, 0, blk * n_pair + hp))
                 for blk in range(N_COL_BLOCKS)]
    args = [xb] + [w_in] * N_COL_BLOCKS
    aliases = {}
    if prevs is not None:
        for g in range(N_GROUPS):
            aliases[len(args)] = 1 + g
            in_specs.append(pl.BlockSpec(memory_space=pl.ANY))
            args.append(prevs[g])
    res = pl.pallas_call(
        functools.partial(_attn_prompt_kernel, seq=seq),
        grid=(nbatch, n_pair),
        in_specs=in_specs,
        out_specs=[pl.BlockSpec((None, seq, LANES), lambda b, hp: (b, 0, hp))] + [kv_spec(k) for k in keeps],
        out_shape=[jax.ShapeDtypeStruct((nbatch, seq, E_ATTN), BF16)]
        + [jax.ShapeDtypeStruct((n_layers, nbatch, 2, E_ATTN, k), F32) for k in keeps],
        scratch_shapes=[pltpu.VMEM((scr_rows[blk // 3] if blk < 3 * N_GROUPS else seq, LANES), F32)
                        for blk in range(N_COL_BLOCKS)]
        + [pltpu.VMEM((scr_rows[g], LANES), F32) for _ in range(3) for g in range(N_GROUPS)],
        input_output_aliases=aliases,
        compiler_params=pltpu.CompilerParams(
            dimension_semantics=("arbitrary", "arbitrary"), vmem_limit_bytes=VMEM_LIMIT),
        name="attn_prompt",
    )(*args)
    return res[0], list(res[1:])


def _sample_biases(window, dil, t_new, n_heads):
    qi = np.arange(n_heads * t_new)[:, None] % t_new
    old_off = np.arange(window)[None, :] - window
    new_off = np.arange(LANES)[None, :] - (LANES - t_new)
    res = []
    for off, slot_ok in ((old_off, old_off < 0), (new_off, new_off >= 0)):
        dist = qi - off
        ok = slot_ok & (dist >= 0) & (dist <= window) & (dist % dil == 0)
        res.append(np.where(ok, 0.0, NEG_INF).astype(np.float32))
    return res


def _attn_sample_kernel(*refs, t_new):
    n_in = 10 + 3 * N_GROUPS
    h_refs = refs[:10]
    c_refs = refs[10:10 + N_GROUPS]
    b_refs = refs[10 + N_GROUPS:n_in]
    y_ref = refs[-(N_GROUPS + 1)]
    o_refs = refs[-N_GROUPS:]

    width = SAMPLE_HEADS * HEAD_DIM
    n_rows = SAMPLE_HEADS * t_new
    row = lax.broadcasted_iota(jnp.int32, (n_rows, width), 0)
    col = lax.broadcasted_iota(jnp.int32, (n_rows, width), 1)
    diag = jnp.right_shift(row, t_new.bit_length() - 1) == jnp.right_shift(col, HEAD_DIM.bit_length() - 1)
    new_lanes = lax.broadcasted_iota(jnp.int32, (width, LANES), 1) >= LANES - t_new
    zpad = jnp.zeros((LANES - t_new, width), F32)

    outs, lses = [], []
    for g, (window, _) in enumerate(GROUPS):
        q = h_refs[3 * g][...] * QK_SCALE
        kt_new = jnp.concatenate([zpad, h_refs[3 * g + 1][...]], axis=0).T
        vt_new = jnp.concatenate([zpad, h_refs[3 * g + 2][...]], axis=0).T
        kt = c_refs[g][0]
        vt = c_refs[g][1]
        for idx, (old, new) in enumerate(((kt, kt_new), (vt, vt_new))):
            rolled = pltpu.roll(old, window - t_new, axis=1)
            if window > LANES:
                o_refs[g][idx, :, 0:window - LANES] = rolled[:, 0:window - LANES]
            o_refs[g][idx, :, window - LANES:window] = jnp.where(new_lanes, new, rolled[:, window - LANES:])

        q_rep = jnp.broadcast_to(q[None], (SAMPLE_HEADS, t_new, width)).reshape(n_rows, width)
        q_bd = jnp.where(diag, q_rep, 0.0).astype(BF16)
        s_old = _dot(q_bd, kt.astype(BF16)) + b_refs[2 * g][...]
        s_new = _dot(q_bd, kt_new.astype(BF16)) + b_refs[2 * g + 1][...]
        m = jnp.maximum(jnp.max(s_old, axis=-1, keepdims=True), jnp.max(s_new, axis=-1, keepdims=True))
        p_old = jnp.exp(s_old - m)
        p_new = jnp.exp(s_new - m)
        l = jnp.sum(p_old, axis=-1, keepdims=True) + jnp.sum(p_new, axis=-1, keepdims=True)
        o = (_dot_nt(p_old.astype(BF16), vt.astype(BF16))
             + _dot_nt(p_new.astype(BF16), vt_new.astype(BF16))) * (1.0 / l)
        lse = m + jnp.log(l)
        outs.append(jnp.where(diag, o, 0.0).reshape(SAMPLE_HEADS, t_new, width).sum(axis=0))
        lses.append(jnp.where(diag, lse, 0.0).reshape(SAMPLE_HEADS, t_new, width).sum(axis=0))

    z = h_refs[9][...]
    y_ref[...] = _merge_groups(outs, lses) * _silu(z)


def _attn_sample(h, caches_t, li, prevs):
    nbatch, t_new, _ = h.shape
    n_layers = caches_t[0].shape[0]
    width = SAMPLE_HEADS * HEAD_DIM
    n_chunk = E_ATTN // width
    assert t_new & (t_new - 1) == 0 and t_new <= SUBLANES

    def h_spec(blk):
        return pl.BlockSpec((None, t_new, width), lambda b, c: (b, 0, blk * n_chunk + c))

    def cache_spec(window):
        return pl.BlockSpec((None, None, 2, width, window), lambda b, c: (li, b, 0, c, 0))

    biases = []
    for window, dil in GROUPS:
        biases += [jnp.asarray(a) for a in _sample_biases(window, dil, t_new, SAMPLE_HEADS)]
    full = lambda a: pl.BlockSpec(a.shape, lambda b, c: (0, 0))

    in_specs = [h_spec(blk) for blk in range(10)]
    in_specs += [cache_spec(window) for window, _ in GROUPS]
    in_specs += [full(a) for a in biases]
    args = [h] * 10 + list(caches_t) + biases
    aliases = {}
    if prevs is not None:
        for g in range(N_GROUPS):
            aliases[len(args)] = 1 + g
            in_specs.append(pl.BlockSpec(memory_space=pl.ANY))
            args.append(prevs[g])
    res = pl.pallas_call(
        functools.partial(_attn_sample_kernel, t_new=t_new),
        grid=(nbatch, n_chunk),
        in_specs=in_specs,
        out_specs=[pl.BlockSpec((None, t_new, width), lambda b, c: (b, 0, c))]
        + [cache_spec(window) for window, _ in GROUPS],
        out_shape=[jax.ShapeDtypeStruct((nbatch, t_new, E_ATTN), F32)]
        + [jax.ShapeDtypeStruct((n_layers, nbatch, 2, E_ATTN, window), F32) for window, _ in GROUPS],
        input_output_aliases=aliases,
        compiler_params=pltpu.CompilerParams(
            dimension_semantics=("arbitrary", "arbitrary"), vmem_limit_bytes=VMEM_LIMIT),
        name="attn_sample",
    )(*args)
    return res[0], list(res[1:])


def _cache_time_minor(c):
    n_layers, nbatch, window = c.shape[:3]
    return jnp.transpose(c, (0, 1, 3, 4, 5, 2)).reshape(n_layers, nbatch, 2, E_ATTN, window)


def _cache_time_major(c):
    n_layers, nbatch, window = c.shape[0], c.shape[1], c.shape[-1]
    c = c.reshape(n_layers, nbatch, 2, N_HEADS, HEAD_DIM, window)
    return jnp.transpose(c, (0, 1, 5, 2, 3, 4))


def kernel(x_prompt, x_sample, cache_kv_w128_d1, cache_kv_w512_d4, cache_kv_w2048_d16, state_conv,
           conv_w_in, conv_dw_w, conv_dw_b, conv_ln_g, conv_ln_b, conv_w_out,
           attn_w_in, attn_w_out, norm_g, norm_b):
    caches_t = [_cache_time_minor(c) for c in (cache_kv_w128_d1, cache_kv_w512_d4, cache_kv_w2048_d16)]
    conv_w_in_b = conv_w_in.astype(BF16)
    conv_w_out_b = conv_w_out.astype(BF16)
    attn_w_out_b = attn_w_out.astype(BF16)

    yp, ys = x_prompt, x_sample
    bp, tp, _ = yp.shape
    bs, ts, _ = ys.shape
    kv_p = None
    kv_s = None
    conv_p, conv_s = [], []
    for i in range(DEPTH):
        li = i // 2
        if i % 2 == 0:
            prm = (conv_w_in_b[li], conv_dw_w[li], conv_dw_b[li], conv_ln_g[li], conv_ln_b[li],
                   conv_w_out_b[li], norm_g[i], norm_b[i])
            zero_buf = jnp.zeros((bp, CONV_WIDTH - 1, E_CONV), F32)
            yp, yp_b, stp = _conv_layer(yp, zero_buf, *prm, emit_bf16=True)
            ys, _, sts = _conv_layer(ys, state_conv[li], *prm, emit_bf16=False)
            conv_p.append(stp)
            conv_s.append(sts)
        else:
            hs = _matmul(ys.reshape(bs * ts, D_MODEL), attn_w_in, li).reshape(bs, ts, -1)
            op, kv_p = _attn_prompt(yp_b, attn_w_in, li, kv_p)
            os_, kv_s = _attn_sample(hs, caches_t, li, kv_s)
            yp = _proj_ln(op.reshape(bp * tp, E_ATTN), yp.reshape(bp * tp, D_MODEL),
                          attn_w_out_b[li], norm_g[i], norm_b[i]).reshape(bp, tp, D_MODEL)
            ys = _proj_ln(os_.reshape(bs * ts, E_ATTN), ys.reshape(bs * ts, D_MODEL),
                          attn_w_out_b[li], norm_g[i], norm_b[i]).reshape(bs, ts, D_MODEL)
    return (yp, ys,
            _cache_time_major(kv_p[0]), _cache_time_major(kv_p[1]), _cache_time_major(kv_p[2]),
            jnp.stack(conv_p),
            _cache_time_major(kv_s[0]), _cache_time_major(kv_s[1]), _cache_time_major(kv_s[2]),
            jnp.stack(conv_s))
```

```python
import functools

import jax
import jax.numpy as jnp
import numpy as np
from jax import lax
from jax.experimental import pallas as pl
from jax.experimental.pallas import tpu as pltpu

D_MODEL = 1024
DEPTH = 4
E_CONV = D_MODEL
CONV_WIDTH = 31
HEAD_DIM = 64
N_HEADS = D_MODEL // HEAD_DIM
E_ATTN = N_HEADS * HEAD_DIM
GROUPS = ((128, 1), (512, 4), (2048, 16))
N_GROUPS = len(GROUPS)
ALPHA = (2.0 * DEPTH) ** 0.25
LN_EPS = 1e-5
NEG_INF = -1e30
QK_SCALE = HEAD_DIM ** -0.5

LANES = 128
SUBLANES = 8
HALO = 32
HALO_OFF = HALO - (CONV_WIDTH - 1)
CONV_COLS = 256
CONV_ROWS = 256
Q_BLK = 128
HEADS_PER_STEP = LANES // HEAD_DIM
SAMPLE_HEADS = 8
VMEM_LIMIT = 56 * 1024 * 1024

BF16 = jnp.bfloat16
F32 = jnp.float32


def _dot(a, b):
    return jnp.dot(a, b, preferred_element_type=F32)


def _dot_nt(a, b):
    return lax.dot_general(a, b, (((1,), (1,)), ((), ())), preferred_element_type=F32)


def _sigmoid(x):
    return 0.5 * jnp.tanh(0.5 * x) + 0.5


def _silu(x):
    h = 0.5 * x
    return h + h * jnp.tanh(h)


def _layer_norm(x, g, b):
    mu = jnp.mean(x, axis=-1, keepdims=True)
    xc = x - mu
    var = jnp.mean(xc * xc, axis=-1, keepdims=True)
    return xc * lax.rsqrt(var + LN_EPS) * g + b


def _merge_groups(outs, lses):
    mx = functools.reduce(jnp.maximum, lses)
    es = [jnp.exp(lv - mx) for lv in lses]
    den = functools.reduce(lambda a, b: a + b, es)
    num = functools.reduce(lambda a, b: a + b, [e * o for e, o in zip(es, outs)])
    return num / den


def _conv_layer_kernel(x_ref, buf_ref, win_ref, dww_ref, dwb_ref, lng_ref, lnb_ref, wout_ref,
                       ng_ref, nb_ref, o_ref, *rest, nb, tt, nt, rc, emit_bf16):
    ob_ref = rest[0] if emit_bf16 else None
    st_ref, ubuf, zbuf, cbuf, shbuf = rest[-5:]
    t = pl.program_id(1)
    n_rows = nb * tt

    @pl.when(t == 0)
    def _():
        ubuf[:, 0:HALO, :] = buf_ref[...]

    x = x_ref[...].reshape(n_rows, D_MODEL)
    xb = x.astype(BF16)

    n_sh = tt + HALO - SUBLANES
    for c0 in range(0, E_CONV, CONV_COLS):
        cols = slice(c0, c0 + CONV_COLS)
        a = _dot(xb, win_ref[:, c0:c0 + CONV_COLS])
        gl = _dot(xb, win_ref[:, E_CONV + c0:E_CONV + c0 + CONV_COLS])
        ubuf[:, HALO:HALO + tt, cols] = (a * _sigmoid(gl)).reshape(nb, tt, CONV_COLS)
        z = _dot(xb, win_ref[:, 2 * E_CONV + c0:2 * E_CONV + c0 + CONV_COLS])
        zbuf[:, cols] = _silu(z)
        for c in range(c0, c0 + CONV_COLS, LANES):
            lanes = slice(c, c + LANES)
            for s in range(1, SUBLANES):
                shbuf[s, :, :, :] = ubuf[:, s:s + n_sh, lanes]
            for r0 in range(0, tt, rc):
                acc = jnp.zeros((nb, rc, LANES), F32)
                for k in range(CONV_WIDTH):
                    lo = HALO_OFF + k + r0
                    s = lo % SUBLANES
                    base = lo - s
                    assert base + rc <= (n_sh if s else tt + HALO)
                    src = shbuf[s, :, base:base + rc, :] if s else ubuf[:, base:base + rc, lanes]
                    acc = acc + dww_ref[k:k + 1, lanes] * src
                cbuf[:, r0:r0 + rc, lanes] = acc

    cn = _layer_norm(cbuf[...].reshape(n_rows, E_CONV) + dwb_ref[...], lng_ref[...], lnb_ref[...])
    y = _silu(cn) * zbuf[...]
    f = _dot(y.astype(BF16), wout_ref[...])
    out = _layer_norm(ALPHA * x + f, ng_ref[...], nb_ref[...])
    o_ref[...] = out.reshape(nb, tt, D_MODEL)
    if emit_bf16:
        ob_ref[...] = out.astype(BF16).reshape(nb, tt, D_MODEL)

    @pl.when(t == nt - 1)
    def _():
        st_ref[...] = ubuf[:, tt:tt + HALO, :]

    if nt > 1:
        ubuf[:, 0:HALO, :] = ubuf[:, tt:tt + HALO, :]


def _conv_layer(x, buf, w_in, dw_w, dw_b, ln_g, ln_b, w_out, ng, nb_, emit_bf16):
    nbatch, seq, _ = x.shape
    tt = min(seq, CONV_ROWS)
    nb = max(1, min(nbatch, CONV_ROWS // seq))
    assert seq % tt == 0 and tt % SUBLANES == 0 and nbatch % nb == 0
    assert nb == 1 or not emit_bf16
    nt = seq // tt
    rc = min(tt, 16)
    buf_pad = jnp.pad(buf, ((0, 0), (HALO_OFF, 0), (0, 0)))
    row = lambda v: v.reshape(1, -1)
    const = lambda shape: pl.BlockSpec(shape, lambda b, t: (0,) * len(shape))
    tile = pl.BlockSpec((nb, tt, D_MODEL), lambda b, t: (b, t, 0))
    halo = pl.BlockSpec((nb, HALO, E_CONV), lambda b, t: (b, 0, 0))
    res = pl.pallas_call(
        functools.partial(_conv_layer_kernel, nb=nb, tt=tt, nt=nt, rc=rc, emit_bf16=emit_bf16),
        grid=(nbatch // nb, nt),
        in_specs=[
            tile, halo,
            const((D_MODEL, 3 * E_CONV)),
            const((CONV_WIDTH, E_CONV)),
            const((1, E_CONV)), const((1, E_CONV)), const((1, E_CONV)),
            const((E_CONV, D_MODEL)),
            const((1, D_MODEL)), const((1, D_MODEL)),
        ],
        out_specs=[tile] + ([tile] if emit_bf16 else []) + [halo],
        out_shape=[jax.ShapeDtypeStruct((nbatch, seq, D_MODEL), F32)]
        + ([jax.ShapeDtypeStruct((nbatch, seq, D_MODEL), BF16)] if emit_bf16 else [])
        + [jax.ShapeDtypeStruct((nbatch, HALO, E_CONV), F32)],
        scratch_shapes=[
            pltpu.VMEM((nb, HALO + tt, E_CONV), F32),
            pltpu.VMEM((nb * tt, E_CONV), F32),
            pltpu.VMEM((nb, tt, E_CONV), F32),
            pltpu.VMEM((SUBLANES, nb, tt + HALO - SUBLANES, LANES), F32),
        ],
        compiler_params=pltpu.CompilerParams(
            dimension_semantics=("arbitrary", "arbitrary"), vmem_limit_bytes=VMEM_LIMIT),
        name="conv_layer",
    )(x, buf_pad, w_in, dw_w, row(dw_b), row(ln_g), row(ln_b), w_out, row(ng), row(nb_))
    return res[0], (res[1] if emit_bf16 else None), res[-1][:, HALO_OFF:, :]


def _matmul_kernel(x_ref, w_ref, o_ref):
    o_ref[...] = _dot(x_ref[...].astype(BF16), w_ref[...].astype(BF16))


def _matmul(x, w, li):
    m, k = x.shape
    n = w.shape[2]
    tm = min(m, 512)
    tn = 1024
    assert m % tm == 0 and n % tn == 0
    return pl.pallas_call(
        _matmul_kernel,
        grid=(m // tm, n // tn),
        in_specs=[pl.BlockSpec((tm, k), lambda i, j: (i, 0)),
                  pl.BlockSpec((None, k, tn), lambda i, j: (li, 0, j))],
        out_specs=pl.BlockSpec((tm, tn), lambda i, j: (i, j)),
        out_shape=jax.ShapeDtypeStruct((m, n), F32),
        compiler_params=pltpu.CompilerParams(
            dimension_semantics=("arbitrary", "arbitrary"), vmem_limit_bytes=VMEM_LIMIT),
        name="attn_in_proj",
    )(x, w)


def _proj_ln_kernel(y_ref, x_ref, w_ref, g_ref, b_ref, o_ref):
    f = _dot(y_ref[...].astype(BF16), w_ref[...])
    o_ref[...] = _layer_norm(ALPHA * x_ref[...] + f, g_ref[...], b_ref[...])


def _proj_ln(y, x, w, g, b):
    m, k = y.shape
    n = w.shape[1]
    tm = min(m, 1024)
    assert m % tm == 0
    return pl.pallas_call(
        _proj_ln_kernel,
        grid=(m // tm,),
        in_specs=[pl.BlockSpec((tm, k), lambda i: (i, 0)),
                  pl.BlockSpec((tm, n), lambda i: (i, 0)),
                  pl.BlockSpec((k, n), lambda i: (0, 0)),
                  pl.BlockSpec((1, n), lambda i: (0, 0)),
                  pl.BlockSpec((1, n), lambda i: (0, 0))],
        out_specs=pl.BlockSpec((tm, n), lambda i: (i, 0)),
        out_shape=jax.ShapeDtypeStruct((m, n), F32),
        compiler_params=pltpu.CompilerParams(
            dimension_semantics=("arbitrary",), vmem_limit_bytes=VMEM_LIMIT),
        name="attn_out_proj_ln",
    )(y, x, w, g.reshape(1, -1), b.reshape(1, -1))


N_COL_BLOCKS = 3 * N_GROUPS + 1
MM_ROWS = 512
MM_BLOCKS = 2
KV_T_COLS = 512
GROUP_PITCH = (1, 4, 24)
UNITS_PER_ITER = 8


def _attn_prompt_kernel(*refs, seq):
    n_scr = N_COL_BLOCKS + 3 * N_GROUPS
    x_ref = refs[0]
    w_refs = refs[1:1 + N_COL_BLOCKS]
    y_ref = refs[-(n_scr + N_GROUPS + 1)]
    kv_refs = refs[-(n_scr + N_GROUPS):-n_scr]
    h_scr = refs[-n_scr:-3 * N_GROUPS]
    o_scr = refs[-3 * N_GROUPS:-2 * N_GROUPS]
    m_scr = refs[-2 * N_GROUPS:-N_GROUPS]
    l_scr = refs[-N_GROUPS:]

    def token_rows(g, t0, n):
        if g >= N_GROUPS or GROUP_PITCH[g] == GROUPS[g][1]:
            return [(t0, n)]
        dil, pitch = GROUPS[g][1], GROUP_PITCH[g]
        base = t0 // dil * pitch
        if not isinstance(base, int):
            base = pl.multiple_of(base, SUBLANES)
        return [(base + jj * pitch, dil) for jj in range(n // dil)]

    def load_tokens(ref, g, t0, n):
        return jnp.concatenate([ref[pl.ds(st, sz), :] for st, sz in token_rows(g, t0, n)], axis=0)

    def store_tokens(ref, g, t0, value):
        lo = 0
        for st, sz in token_rows(g, t0, value.shape[0]):
            ref[pl.ds(st, sz), :] = value[lo:lo + sz]
            lo += sz

    def project(i, carry):
        t0 = pl.multiple_of(i * MM_ROWS, MM_ROWS)
        xb = x_ref[pl.ds(t0, MM_ROWS), :]
        for c in range(0, N_COL_BLOCKS, MM_BLOCKS):
            w = jnp.concatenate([w_refs[c + j][...] for j in range(MM_BLOCKS)], axis=1).astype(BF16)
            r = _dot(xb, w)
            for j in range(MM_BLOCKS):
                store_tokens(h_scr[c + j], (c + j) // 3, t0, r[:, j * LANES:(j + 1) * LANES])
        return carry

    lax.fori_loop(0, seq // MM_ROWS, project, 0)

    for g, (window, _) in enumerate(GROUPS):
        keep = min(window, seq)
        ch = min(keep, KV_T_COLS)
        for kv in range(2):
            src = h_scr[3 * g + 1 + kv]
            for c in range(keep // ch):
                lo = seq - keep + c * ch
                kv_refs[g][kv, :, c * ch:(c + 1) * ch] = load_tokens(src, g, lo, ch).T

    n_stack = HEADS_PER_STEP * Q_BLK
    lane = lax.broadcasted_iota(jnp.int32, (Q_BLK, LANES), 1)
    first_head = lane < HEAD_DIM
    qi = lax.broadcasted_iota(jnp.int32, (n_stack, Q_BLK), 0) & (Q_BLK - 1)
    kj = lax.broadcasted_iota(jnp.int32, (n_stack, Q_BLK), 1)
    bias_cur = jnp.where(kj <= qi, 0.0, NEG_INF)
    bias_prev = jnp.where(kj >= qi, 0.0, NEG_INF)
    bias_both = jnp.concatenate([bias_prev, bias_cur], axis=1)
    key_row = lax.broadcasted_iota(jnp.int32, (n_stack, LANES), 0)
    key_lane = lax.broadcasted_iota(jnp.int32, (n_stack, LANES), 1)
    ones_all = jnp.ones((n_stack, LANES), BF16)
    ones_head = jnp.where((key_row < Q_BLK) == (key_lane < HEAD_DIM), 1.0, 0.0).astype(BF16)

    def rows(start, g):
        if GROUP_PITCH[g] == 1:
            return pl.ds(start, Q_BLK)
        return pl.ds(start, Q_BLK, stride=GROUP_PITCH[g])

    def unit_batch(g, dil, chains):
        q_ref, k_ref, v_ref = h_scr[3 * g: 3 * g + 3]
        for starts, lead_start in chains:
            if lead_start is None:
                k0 = k_ref[rows(starts[0], g), :]
                v0 = v_ref[rows(starts[0], g), :]
                k_bd = jnp.concatenate([jnp.where(first_head, k0, 0.0), jnp.where(first_head, 0.0, k0)],
                                       axis=0).astype(BF16)
                v_bd = jnp.concatenate([jnp.where(first_head, v0, 0.0), jnp.where(first_head, 0.0, v0)],
                                       axis=0).astype(BF16)
                v_bd = jnp.concatenate([v_bd, ones_head], axis=1)
                ks, vs = [k0.astype(BF16)], [v0.astype(BF16)]
            else:
                ks, vs = [], []
                k_lead = k_ref[rows(lead_start, g), :].astype(BF16)
                v_lead = v_ref[rows(lead_start, g), :].astype(BF16)
            ks += [k_ref[rows(st, g), :].astype(BF16) for st in starts[len(ks):]]
            vs += [v_ref[rows(st, g), :].astype(BF16) for st in starts[len(vs):]]
            for a, st in enumerate(starts):
                q = q_ref[rows(st, g), :] * QK_SCALE
                if a > 0 or lead_start is not None:
                    qs = jnp.concatenate([jnp.where(first_head, q, 0.0), jnp.where(first_head, 0.0, q)],
                                         axis=0).astype(BF16)
                    kp, vp = (ks[a - 1], vs[a - 1]) if a > 0 else (k_lead, v_lead)
                    s = _dot_nt(qs, jnp.concatenate([kp, ks[a]], axis=0)) + bias_both
                    m = jnp.max(s, axis=-1, keepdims=True)
                    p = jnp.exp(s - m)
                    vv = jnp.concatenate([jnp.concatenate([vp, vs[a]], axis=0), ones_all], axis=1)
                    ol = _dot(p.astype(BF16), vv)
                    o = jnp.where(first_head, ol[0:Q_BLK, 0:LANES], ol[Q_BLK:, 0:LANES])
                    l = jnp.where(first_head, ol[0:Q_BLK, LANES:], ol[Q_BLK:, LANES:])
                    m0, m1 = m[0:Q_BLK], m[Q_BLK:]
                else:
                    s = _dot_nt(q.astype(BF16), k_bd)
                    s0 = s[:, 0:Q_BLK] + bias_cur[0:Q_BLK]
                    s1 = s[:, Q_BLK:] + bias_cur[0:Q_BLK]
                    m0 = jnp.max(s0, axis=-1, keepdims=True)
                    m1 = jnp.max(s1, axis=-1, keepdims=True)
                    p0 = jnp.exp(s0 - m0)
                    p1 = jnp.exp(s1 - m1)
                    ol = _dot(jnp.concatenate([p0, p1], axis=1).astype(BF16), v_bd)
                    o, l = ol[:, 0:LANES], ol[:, LANES:]
                o_scr[g][rows(st, g), :] = o
                m_scr[g][rows(st, g), :] = jnp.where(first_head, m0, m1)
                l_scr[g][rows(st, g), :] = l

    for g, (_, dil) in enumerate(GROUPS):
        n_blk = seq // (dil * Q_BLK)
        step = GROUP_PITCH[g] * Q_BLK
        if n_blk <= UNITS_PER_ITER:
            n_cls = UNITS_PER_ITER // n_blk
            assert dil % n_cls == 0

            def classes(i, carry, g=g, dil=dil, n_blk=n_blk, step=step, n_cls=n_cls):
                unit_batch(g, dil, [([i * n_cls + c + a * step for a in range(n_blk)], None)
                                    for c in range(n_cls)])
                return carry

            lax.fori_loop(0, dil // n_cls, classes, 0)
        else:
            assert n_blk % UNITS_PER_ITER == 0

            def per_class(r, carry, g=g, dil=dil, n_blk=n_blk, step=step):
                unit_batch(g, dil, [([r + a * step for a in range(UNITS_PER_ITER)], None)])

                def later(i, c2):
                    s0 = r + i * (UNITS_PER_ITER * step)
                    unit_batch(g, dil, [([s0 + a * step for a in range(UNITS_PER_ITER)], s0 - step)])
                    return c2

                lax.fori_loop(1, n_blk // UNITS_PER_ITER, later, 0)
                return carry

            lax.fori_loop(0, dil, per_class, 0)

    chunk = 256
    z_scr = h_scr[N_COL_BLOCKS - 1]

    def merge(i, carry):
        t0 = pl.multiple_of(i * chunk, chunk)
        ms = [load_tokens(m_scr[g], g, t0, chunk) for g in range(N_GROUPS)]
        mx = functools.reduce(jnp.maximum, ms)
        es = [jnp.exp(mv - mx) for mv in ms]
        num = functools.reduce(lambda a, b: a + b,
                               [es[g] * load_tokens(o_scr[g], g, t0, chunk) for g in range(N_GROUPS)])
        den = functools.reduce(lambda a, b: a + b,
                               [es[g] * load_tokens(l_scr[g], g, t0, chunk) for g in range(N_GROUPS)])
        z = z_scr[pl.ds(t0, chunk), :]
        y_ref[pl.ds(t0, chunk), :] = ((num / den) * _silu(z)).astype(BF16)
        return carry

    lax.fori_loop(0, seq // chunk, merge, 0)


def _attn_prompt(xb, w_in, li, prevs):
    nbatch, seq, _ = xb.shape
    n_layers = w_in.shape[0]
    n_pair = E_ATTN // LANES
    assert seq % (GROUPS[-1][1] * Q_BLK) == 0 and seq % MM_ROWS == 0 and N_COL_BLOCKS % MM_BLOCKS == 0
    keeps = [min(window, seq) for window, _ in GROUPS]
    scr_rows = [seq // dil * pitch for (_, dil), pitch in zip(GROUPS, GROUP_PITCH)]

    def kv_spec(keep):
        return pl.BlockSpec((None, None, 2, LANES, keep), lambda b, hp: (li, b, 0, hp, 0))

    in_specs = [pl.BlockSpec((None, seq, D_MODEL), lambda b, hp: (b, 0, 0))]
    in_specs += [pl.BlockSpec((None, D_MODEL, LANES), lambda b, hp, blk=blk: (li, 0, blk * n_pair + hp))
                 for blk in range(N_COL_BLOCKS)]
    args = [xb] + [w_in] * N_COL_BLOCKS
    aliases = {}
    if prevs is not None:
        for g in range(N_GROUPS):
            aliases[len(args)] = 1 + g
            in_specs.append(pl.BlockSpec(memory_space=pl.ANY))
            args.append(prevs[g])
    res = pl.pallas_call(
        functools.partial(_attn_prompt_kernel, seq=seq),
        grid=(nbatch, n_pair),
        in_specs=in_specs,
        out_specs=[pl.BlockSpec((None, seq, LANES), lambda b, hp: (b, 0, hp))] + [kv_spec(k) for k in keeps],
        out_shape=[jax.ShapeDtypeStruct((nbatch, seq, E_ATTN), BF16)]
        + [jax.ShapeDtypeStruct((n_layers, nbatch, 2, E_ATTN, k), F32) for k in keeps],
        scratch_shapes=[pltpu.VMEM((scr_rows[blk // 3] if blk < 3 * N_GROUPS else seq, LANES), F32)
                        for blk in range(N_COL_BLOCKS)]
        + [pltpu.VMEM((scr_rows[g], LANES), F32) for _ in range(3) for g in range(N_GROUPS)],
        input_output_aliases=aliases,
        compiler_params=pltpu.CompilerParams(
            dimension_semantics=("arbitrary", "arbitrary"), vmem_limit_bytes=VMEM_LIMIT),
        name="attn_prompt",
    )(*args)
    return res[0], list(res[1:])


def _sample_biases(window, dil, t_new, n_heads):
    qi = np.arange(n_heads * t_new)[:, None] % t_new
    old_off = np.arange(window)[None, :] - window
    new_off = np.arange(LANES)[None, :] - (LANES - t_new)
    res = []
    for off, slot_ok in ((old_off, old_off < 0), (new_off, new_off >= 0)):
        dist = qi - off
        ok = slot_ok & (dist >= 0) & (dist <= window) & (dist % dil == 0)
        res.append(np.where(ok, 0.0, NEG_INF).astype(np.float32))
    return res


def _attn_sample_kernel(*refs, t_new):
    n_in = 10 + 3 * N_GROUPS
    h_refs = refs[:10]
    c_refs = refs[10:10 + N_GROUPS]
    b_refs = refs[10 + N_GROUPS:n_in]
    y_ref = refs[-(N_GROUPS + 1)]
    o_refs = refs[-N_GROUPS:]

    width = SAMPLE_HEADS * HEAD_DIM
    n_rows = SAMPLE_HEADS * t_new
    row = lax.broadcasted_iota(jnp.int32, (n_rows, width), 0)
    col = lax.broadcasted_iota(jnp.int32, (n_rows, width), 1)
    diag = jnp.right_shift(row, t_new.bit_length() - 1) == jnp.right_shift(col, HEAD_DIM.bit_length() - 1)
    new_lanes = lax.broadcasted_iota(jnp.int32, (width, LANES), 1) >= LANES - t_new
    zpad = jnp.zeros((LANES - t_new, width), F32)

    outs, lses = [], []
    for g, (window, _) in enumerate(GROUPS):
        q = h_refs[3 * g][...] * QK_SCALE
        kt_new = jnp.concatenate([zpad, h_refs[3 * g + 1][...]], axis=0).T
        vt_new = jnp.concatenate([zpad, h_refs[3 * g + 2][...]], axis=0).T
        kt = c_refs[g][0]
        vt = c_refs[g][1]
        for idx, (old, new) in enumerate(((kt, kt_new), (vt, vt_new))):
            rolled = pltpu.roll(old, window - t_new, axis=1)
            if window > LANES:
                o_refs[g][idx, :, 0:window - LANES] = rolled[:, 0:window - LANES]
            o_refs[g][idx, :, window - LANES:window] = jnp.where(new_lanes, new, rolled[:, window - LANES:])

        q_rep = jnp.broadcast_to(q[None], (SAMPLE_HEADS, t_new, width)).reshape(n_rows, width)
        q_bd = jnp.where(diag, q_rep, 0.0).astype(BF16)
        s_old = _dot(q_bd, kt.astype(BF16)) + b_refs[2 * g][...]
        s_new = _dot(q_bd, kt_new.astype(BF16)) + b_refs[2 * g + 1][...]
        m = jnp.maximum(jnp.max(s_old, axis=-1, keepdims=True), jnp.max(s_new, axis=-1, keepdims=True))
        p_old = jnp.exp(s_old - m)
        p_new = jnp.exp(s_new - m)
        l = jnp.sum(p_old, axis=-1, keepdims=True) + jnp.sum(p_new, axis=-1, keepdims=True)
        o = (_dot_nt(p_old.astype(BF16), vt.astype(BF16))
             + _dot_nt(p_new.astype(BF16), vt_new.astype(BF16))) * (1.0 / l)
        lse = m + jnp.log(l)
        outs.append(jnp.where(diag, o, 0.0).reshape(SAMPLE_HEADS, t_new, width).sum(axis=0))
        lses.append(jnp.where(diag, lse, 0.0).reshape(SAMPLE_HEADS, t_new, width).sum(axis=0))

    z = h_refs[9][...]
    y_ref[...] = _merge_groups(outs, lses) * _silu(z)


def _attn_sample(h, caches_t, li, prevs):
    nbatch, t_new, _ = h.shape
    n_layers = caches_t[0].shape[0]
    width = SAMPLE_HEADS * HEAD_DIM
    n_chunk = E_ATTN // width
    assert t_new & (t_new - 1) == 0 and t_new <= SUBLANES

    def h_spec(blk):
        return pl.BlockSpec((None, t_new, width), lambda b, c: (b, 0, blk * n_chunk + c))

    def cache_spec(window):
        return pl.BlockSpec((None, None, 2, width, window), lambda b, c: (li, b, 0, c, 0))

    biases = []
    for window, dil in GROUPS:
        biases += [jnp.asarray(a) for a in _sample_biases(window, dil, t_new, SAMPLE_HEADS)]
    full = lambda a: pl.BlockSpec(a.shape, lambda b, c: (0, 0))

    in_specs = [h_spec(blk) for blk in range(10)]
    in_specs += [cache_spec(window) for window, _ in GROUPS]
    in_specs += [full(a) for a in biases]
    args = [h] * 10 + list(caches_t) + biases
    aliases = {}
    if prevs is not None:
        for g in range(N_GROUPS):
            aliases[len(args)] = 1 + g
            in_specs.append(pl.BlockSpec(memory_space=pl.ANY))
            args.append(prevs[g])
    res = pl.pallas_call(
        functools.partial(_attn_sample_kernel, t_new=t_new),
        grid=(nbatch, n_chunk),
        in_specs=in_specs,
        out_specs=[pl.BlockSpec((None, t_new, width), lambda b, c: (b, 0, c))]
        + [cache_spec(window) for window, _ in GROUPS],
        out_shape=[jax.ShapeDtypeStruct((nbatch, t_new, E_ATTN), F32)]
        + [jax.ShapeDtypeStruct((n_layers, nbatch, 2, E_ATTN, window), F32) for window, _ in GROUPS],
        input_output_aliases=aliases,
        compiler_params=pltpu.CompilerParams(
            dimension_semantics=("arbitrary", "arbitrary"), vmem_limit_bytes=VMEM_LIMIT),
        name="attn_sample",
    )(*args)
    return res[0], list(res[1:])


def _cache_time_minor(c):
    n_layers, nbatch, window = c.shape[:3]
    return jnp.transpose(c, (0, 1, 3, 4, 5, 2)).reshape(n_layers, nbatch, 2, E_ATTN, window)


def _cache_time_major(c):
    n_layers, nbatch, window = c.shape[0], c.shape[1], c.shape[-1]
    c = c.reshape(n_layers, nbatch, 2, N_HEADS, HEAD_DIM, window)
    return jnp.transpose(c, (0, 1, 5, 2, 3, 4))


def kernel(x_prompt, x_sample, cache_kv_w128_d1, cache_kv_w512_d4, cache_kv_w2048_d16, state_conv,
           conv_w_in, conv_dw_w, conv_dw_b, conv_ln_g, conv_ln_b, conv_w_out,
           attn_w_in, attn_w_out, norm_g, norm_b):
    caches_t = [_cache_time_minor(c) for c in (cache_kv_w128_d1, cache_kv_w512_d4, cache_kv_w2048_d16)]
    conv_w_in_b = conv_w_in.astype(BF16)
    conv_w_out_b = conv_w_out.astype(BF16)
    attn_w_out_b = attn_w_out.astype(BF16)

    yp, ys = x_prompt, x_sample
    bp, tp, _ = yp.shape
    bs, ts, _ = ys.shape
    kv_p = None
    kv_s = None
    conv_p, conv_s = [], []
    for i in range(DEPTH):
        li = i // 2
        if i % 2 == 0:
            prm = (conv_w_in_b[li], conv_dw_w[li], conv_dw_b[li], conv_ln_g[li], conv_ln_b[li],
                   conv_w_out_b[li], norm_g[i], norm_b[i])
            zero_buf = jnp.zeros((bp, CONV_WIDTH - 1, E_CONV), F32)
            yp, yp_b, stp = _conv_layer(yp, zero_buf, *prm, emit_bf16=True)
            ys, _, sts = _conv_layer(ys, state_conv[li], *prm, emit_bf16=False)
            conv_p.append(stp)
            conv_s.append(sts)
        else:
            hs = _matmul(ys.reshape(bs * ts, D_MODEL), attn_w_in, li).reshape(bs, ts, -1)
            op, kv_p = _attn_prompt(yp_b, attn_w_in, li, kv_p)
            os_, kv_s = _attn_sample(hs, caches_t, li, kv_s)
            yp = _proj_ln(op.reshape(bp * tp, E_ATTN), yp.reshape(bp * tp, D_MODEL),
                          attn_w_out_b[li], norm_g[i], norm_b[i]).reshape(bp, tp, D_MODEL)
            ys = _proj_ln(os_.reshape(bs * ts, E_ATTN), ys.reshape(bs * ts, D_MODEL),
                          attn_w_out_b[li], norm_g[i], norm_b[i]).reshape(bs, ts, D_MODEL)
    return (yp, ys,
            _cache_time_major(kv_p[0]), _cache_time_major(kv_p[1]), _cache_time_major(kv_p[2]),
            jnp.stack(conv_p),
            _cache_time_major(kv_s[0]), _cache_time_major(kv_s[1]), _cache_time_major(kv_s[2]),
            jnp.stack(conv_s))
```

```python
import functools

import jax
import jax.numpy as jnp
import numpy as np
from jax import lax
from jax.experimental import pallas as pl
from jax.experimental.pallas import tpu as pltpu

D_MODEL = 1024
DEPTH = 4
E_CONV = D_MODEL
CONV_WIDTH = 31
HEAD_DIM = 64
N_HEADS = D_MODEL // HEAD_DIM
E_ATTN = N_HEADS * HEAD_DIM
GROUPS = ((128, 1), (512, 4), (2048, 16))
N_GROUPS = len(GROUPS)
ALPHA = (2.0 * DEPTH) ** 0.25
LN_EPS = 1e-5
NEG_INF = -1e30
QK_SCALE = HEAD_DIM ** -0.5

LANES = 128
SUBLANES = 8
HALO = 32
HALO_OFF = HALO - (CONV_WIDTH - 1)
CONV_COLS = 256
CONV_ROWS = 512
Q_BLK = 128
HEADS_PER_STEP = LANES // HEAD_DIM
SAMPLE_HEADS = 8
VMEM_LIMIT = 56 * 1024 * 1024

BF16 = jnp.bfloat16
F32 = jnp.float32


def _dot(a, b):
    return jnp.dot(a, b, preferred_element_type=F32)


def _dot_nt(a, b):
    return lax.dot_general(a, b, (((1,), (1,)), ((), ())), preferred_element_type=F32)


def _sigmoid(x):
    return 0.5 * jnp.tanh(0.5 * x) + 0.5


def _silu(x):
    h = 0.5 * x
    return h + h * jnp.tanh(h)


def _layer_norm(x, g, b):
    mu = jnp.mean(x, axis=-1, keepdims=True)
    xc = x - mu
    var = jnp.mean(xc * xc, axis=-1, keepdims=True)
    return xc * lax.rsqrt(var + LN_EPS) * g + b


def _merge_groups(outs, lses):
    mx = functools.reduce(jnp.maximum, lses)
    es = [jnp.exp(lv - mx) for lv in lses]
    den = functools.reduce(lambda a, b: a + b, es)
    num = functools.reduce(lambda a, b: a + b, [e * o for e, o in zip(es, outs)])
    return num / den


def _conv_layer_kernel(x_ref, buf_ref, win_ref, dww_ref, dwb_ref, lng_ref, lnb_ref, wout_ref,
                       ng_ref, nb_ref, o_ref, *rest, nb, tt, nt, rc, emit_bf16):
    ob_ref = rest[0] if emit_bf16 else None
    st_ref, ubuf, zbuf, cbuf, shbuf = rest[-5:]
    t = pl.program_id(1)
    n_rows = nb * tt

    @pl.when(t == 0)
    def _():
        ubuf[:, 0:HALO, :] = buf_ref[...]

    x = x_ref[...].reshape(n_rows, D_MODEL)
    xb = x.astype(BF16)

    n_sh = tt + HALO - SUBLANES
    for c0 in range(0, E_CONV, CONV_COLS):
        cols = slice(c0, c0 + CONV_COLS)
        a = _dot(xb, win_ref[:, c0:c0 + CONV_COLS])
        gl = _dot(xb, win_ref[:, E_CONV + c0:E_CONV + c0 + CONV_COLS])
        ubuf[:, HALO:HALO + tt, cols] = (a * _sigmoid(gl)).reshape(nb, tt, CONV_COLS)
        z = _dot(xb, win_ref[:, 2 * E_CONV + c0:2 * E_CONV + c0 + CONV_COLS])
        zbuf[:, cols] = _silu(z)
        for c in range(c0, c0 + CONV_COLS, LANES):
            lanes = slice(c, c + LANES)
            for s in range(1, SUBLANES):
                shbuf[s, :, :, :] = ubuf[:, s:s + n_sh, lanes]
            for r0 in range(0, tt, rc):
                acc = jnp.zeros((nb, rc, LANES), F32)
                for k in range(CONV_WIDTH):
                    lo = HALO_OFF + k + r0
                    s = lo % SUBLANES
                    base = lo - s
                    assert base + rc <= (n_sh if s else tt + HALO)
                    src = shbuf[s, :, base:base + rc, :] if s else ubuf[:, base:base + rc, lanes]
                    acc = acc + dww_ref[k:k + 1, lanes] * src
                cbuf[:, r0:r0 + rc, lanes] = acc

    cn = _layer_norm(cbuf[...].reshape(n_rows, E_CONV) + dwb_ref[...], lng_ref[...], lnb_ref[...])
    y = _silu(cn) * zbuf[...]
    f = _dot(y.astype(BF16), wout_ref[...])
    out = _layer_norm(ALPHA * x + f, ng_ref[...], nb_ref[...])
    o_ref[...] = out.reshape(nb, tt, D_MODEL)
    if emit_bf16:
        ob_ref[...] = out.astype(BF16).reshape(nb, tt, D_MODEL)

    @pl.when(t == nt - 1)
    def _():
        st_ref[...] = ubuf[:, tt:tt + HALO, :]

    if nt > 1:
        ubuf[:, 0:HALO, :] = ubuf[:, tt:tt + HALO, :]


def _conv_layer(x, buf, w_in, dw_w, dw_b, ln_g, ln_b, w_out, ng, nb_, emit_bf16):
    nbatch, seq, _ = x.shape
    tt = min(seq, CONV_ROWS)
    nb = max(1, min(nbatch, CONV_ROWS // seq))
    assert seq % tt == 0 and tt % SUBLANES == 0 and nbatch % nb == 0
    assert nb == 1 or not emit_bf16
    nt = seq // tt
    rc = min(tt, 16)
    buf_pad = jnp.pad(buf, ((0, 0), (HALO_OFF, 0), (0, 0)))
    row = lambda v: v.reshape(1, -1)
    const = lambda shape: pl.BlockSpec(shape, lambda b, t: (0,) * len(shape))
    tile = pl.BlockSpec((nb, tt, D_MODEL), lambda b, t: (b, t, 0))
    halo = pl.BlockSpec((nb, HALO, E_CONV), lambda b, t: (b, 0, 0))
    res = pl.pallas_call(
        functools.partial(_conv_layer_kernel, nb=nb, tt=tt, nt=nt, rc=rc, emit_bf16=emit_bf16),
        grid=(nbatch // nb, nt),
        in_specs=[
            tile, halo,
            const((D_MODEL, 3 * E_CONV)),
            const((CONV_WIDTH, E_CONV)),
            const((1, E_CONV)), const((1, E_CONV)), const((1, E_CONV)),
            const((E_CONV, D_MODEL)),
            const((1, D_MODEL)), const((1, D_MODEL)),
        ],
        out_specs=[tile] + ([tile] if emit_bf16 else []) + [halo],
        out_shape=[jax.ShapeDtypeStruct((nbatch, seq, D_MODEL), F32)]
        + ([jax.ShapeDtypeStruct((nbatch, seq, D_MODEL), BF16)] if emit_bf16 else [])
        + [jax.ShapeDtypeStruct((nbatch, HALO, E_CONV), F32)],
        scratch_shapes=[
            pltpu.VMEM((nb, HALO + tt, E_CONV), F32),
            pltpu.VMEM((nb * tt, E_CONV), F32),
            pltpu.VMEM((nb, tt, E_CONV), F32),
            pltpu.VMEM((SUBLANES, nb, tt + HALO - SUBLANES, LANES), F32),
        ],
        compiler_params=pltpu.CompilerParams(
            dimension_semantics=("arbitrary", "arbitrary"), vmem_limit_bytes=VMEM_LIMIT),
        name="conv_layer",
    )(x, buf_pad, w_in, dw_w, row(dw_b), row(ln_g), row(ln_b), w_out, row(ng), row(nb_))
    return res[0], (res[1] if emit_bf16 else None), res[-1][:, HALO_OFF:, :]


def _matmul_kernel(x_ref, w_ref, o_ref):
    o_ref[...] = _dot(x_ref[...].astype(BF16), w_ref[...].astype(BF16))


def _matmul(x, w, li):
    m, k = x.shape
    n = w.shape[2]
    tm = min(m, 512)
    tn = 1024
    assert m % tm == 0 and n % tn == 0
    return pl.pallas_call(
        _matmul_kernel,
        grid=(m // tm, n // tn),
        in_specs=[pl.BlockSpec((tm, k), lambda i, j: (i, 0)),
                  pl.BlockSpec((None, k, tn), lambda i, j: (li, 0, j))],
        out_specs=pl.BlockSpec((tm, tn), lambda i, j: (i, j)),
        out_shape=jax.ShapeDtypeStruct((m, n), F32),
        compiler_params=pltpu.CompilerParams(
            dimension_semantics=("arbitrary", "arbitrary"), vmem_limit_bytes=VMEM_LIMIT),
        name="attn_in_proj",
    )(x, w)


def _proj_ln_kernel(y_ref, x_ref, w_ref, g_ref, b_ref, o_ref):
    f = _dot(y_ref[...].astype(BF16), w_ref[...])
    o_ref[...] = _layer_norm(ALPHA * x_ref[...] + f, g_ref[...], b_ref[...])


def _proj_ln(y, x, w, g, b):
    m, k = y.shape
    n = w.shape[1]
    tm = min(m, 1024)
    assert m % tm == 0
    return pl.pallas_call(
        _proj_ln_kernel,
        grid=(m // tm,),
        in_specs=[pl.BlockSpec((tm, k), lambda i: (i, 0)),
                  pl.BlockSpec((tm, n), lambda i: (i, 0)),
                  pl.BlockSpec((k, n), lambda i: (0, 0)),
                  pl.BlockSpec((1, n), lambda i: (0, 0)),
                  pl.BlockSpec((1, n), lambda i: (0, 0))],
        out_specs=pl.BlockSpec((tm, n), lambda i: (i, 0)),
        out_shape=jax.ShapeDtypeStruct((m, n), F32),
        compiler_params=pltpu.CompilerParams(
            dimension_semantics=("arbitrary",), vmem_limit_bytes=VMEM_LIMIT),
        name="attn_out_proj_ln",
    )(y, x, w, g.reshape(1, -1), b.reshape(1, -1))


N_COL_BLOCKS = 3 * N_GROUPS + 1
MM_ROWS = 1024
MM_BLOCKS = 2
KV_T_COLS = 512
GROUP_PITCH = (1, 4, 24)
UNITS_PER_ITER = 8


def _attn_prompt_kernel(*refs, seq):
    n_scr = N_COL_BLOCKS + 3 * N_GROUPS
    x_ref = refs[0]
    w_refs = refs[1:1 + N_COL_BLOCKS]
    y_ref = refs[-(n_scr + N_GROUPS + 1)]
    kv_refs = refs[-(n_scr + N_GROUPS):-n_scr]
    h_scr = refs[-n_scr:-3 * N_GROUPS]
    o_scr = refs[-3 * N_GROUPS:-2 * N_GROUPS]
    m_scr = refs[-2 * N_GROUPS:-N_GROUPS]
    l_scr = refs[-N_GROUPS:]

    def token_rows(g, t0, n):
        if g >= N_GROUPS or GROUP_PITCH[g] == GROUPS[g][1]:
            return [(t0, n)]
        dil, pitch = GROUPS[g][1], GROUP_PITCH[g]
        base = t0 // dil * pitch
        if not isinstance(base, int):
            base = pl.multiple_of(base, SUBLANES)
        return [(base + jj * pitch, dil) for jj in range(n // dil)]

    def load_tokens(ref, g, t0, n):
        return jnp.concatenate([ref[pl.ds(st, sz), :] for st, sz in token_rows(g, t0, n)], axis=0)

    def store_tokens(ref, g, t0, value):
        lo = 0
        for st, sz in token_rows(g, t0, value.shape[0]):
            ref[pl.ds(st, sz), :] = value[lo:lo + sz]
            lo += sz

    def project(i, carry):
        t0 = pl.multiple_of(i * MM_ROWS, MM_ROWS)
        xb = x_ref[pl.ds(t0, MM_ROWS), :]
        for c in range(0, N_COL_BLOCKS, MM_BLOCKS):
            w = jnp.concatenate([w_refs[c + j][...] for j in range(MM_BLOCKS)], axis=1).astype(BF16)
            r = _dot(xb, w)
            for j in range(MM_BLOCKS):
                store_tokens(h_scr[c + j], (c + j) // 3, t0, r[:, j * LANES:(j + 1) * LANES])
        return carry

    lax.fori_loop(0, seq // MM_ROWS, project, 0)

    for g, (window, _) in enumerate(GROUPS):
        keep = min(window, seq)
        ch = min(keep, KV_T_COLS)
        for kv in range(2):
            src = h_scr[3 * g + 1 + kv]
            for c in range(keep // ch):
                lo = seq - keep + c * ch
                kv_refs[g][kv, :, c * ch:(c + 1) * ch] = load_tokens(src, g, lo, ch).T

    n_stack = HEADS_PER_STEP * Q_BLK
    lane = lax.broadcasted_iota(jnp.int32, (Q_BLK, LANES), 1)
    first_head = lane < HEAD_DIM
    qi = lax.broadcasted_iota(jnp.int32, (n_stack, Q_BLK), 0) & (Q_BLK - 1)
    kj = lax.broadcasted_iota(jnp.int32, (n_stack, Q_BLK), 1)
    bias_cur = jnp.where(kj <= qi, 0.0, NEG_INF)
    bias_prev = jnp.where(kj >= qi, 0.0, NEG_INF)
    bias_both = jnp.concatenate([bias_prev, bias_cur], axis=1)
    key_row = lax.broadcasted_iota(jnp.int32, (n_stack, LANES), 0)
    key_lane = lax.broadcasted_iota(jnp.int32, (n_stack, LANES), 1)
    ones_all = jnp.ones((n_stack, LANES), BF16)
    ones_head = jnp.where((key_row < Q_BLK) == (key_lane < HEAD_DIM), 1.0, 0.0).astype(BF16)

    def rows(start, g):
        if GROUP_PITCH[g] == 1:
            return pl.ds(start, Q_BLK)
        return pl.ds(start, Q_BLK, stride=GROUP_PITCH[g])

    def unit_batch(g, dil, chains):
        q_ref, k_ref, v_ref = h_scr[3 * g: 3 * g + 3]
        for starts, lead_start in chains:
            if lead_start is None:
                k0 = k_ref[rows(starts[0], g), :]
                v0 = v_ref[rows(starts[0], g), :]
                k_bd = jnp.concatenate([jnp.where(first_head, k0, 0.0), jnp.where(first_head, 0.0, k0)],
                                       axis=0).astype(BF16)
                v_bd = jnp.concatenate([jnp.where(first_head, v0, 0.0), jnp.where(first_head, 0.0, v0)],
                                       axis=0).astype(BF16)
                v_bd = jnp.concatenate([v_bd, ones_head], axis=1)
                ks, vs = [k0.astype(BF16)], [v0.astype(BF16)]
            else:
                ks, vs = [], []
                k_lead = k_ref[rows(lead_start, g), :].astype(BF16)
                v_lead = v_ref[rows(lead_start, g), :].astype(BF16)
            ks += [k_ref[rows(st, g), :].astype(BF16) for st in starts[len(ks):]]
            vs += [v_ref[rows(st, g), :].astype(BF16) for st in starts[len(vs):]]
            for a, st in enumerate(starts):
                q = q_ref[rows(st, g), :] * QK_SCALE
                if a > 0 or lead_start is not None:
                    qs = jnp.concatenate([jnp.where(first_head, q, 0.0), jnp.where(first_head, 0.0, q)],
                                         axis=0).astype(BF16)
                    kp, vp = (ks[a - 1], vs[a - 1]) if a > 0 else (k_lead, v_lead)
                    s = _dot_nt(qs, jnp.concatenate([kp, ks[a]], axis=0)) + bias_both
                    m = jnp.max(s, axis=-1, keepdims=True)
                    p = jnp.exp(s - m)
                    vv = jnp.concatenate([jnp.concatenate([vp, vs[a]], axis=0), ones_all], axis=1)
                    ol = _dot(p.astype(BF16), vv)
                    o = jnp.where(first_head, ol[0:Q_BLK, 0:LANES], ol[Q_BLK:, 0:LANES])
                    l = jnp.where(first_head, ol[0:Q_BLK, LANES:], ol[Q_BLK:, LANES:])
                    m0, m1 = m[0:Q_BLK], m[Q_BLK:]
                else:
                    s = _dot_nt(q.astype(BF16), k_bd)
                    s0 = s[:, 0:Q_BLK] + bias_cur[0:Q_BLK]
                    s1 = s[:, Q_BLK:] + bias_cur[0:Q_BLK]
                    m0 = jnp.max(s0, axis=-1, keepdims=True)
                    m1 = jnp.max(s1, axis=-1, keepdims=True)
                    p0 = jnp.exp(s0 - m0)
                    p1 = jnp.exp(s1 - m1)
                    ol = _dot(jnp.concatenate([p0, p1], axis=1).astype(BF16), v_bd)
                    o, l = ol[:, 0:LANES], ol[:, LANES:]
                o_scr[g][rows(st, g), :] = o
                m_scr[g][rows(st, g), :] = jnp.where(first_head, m0, m1)
                l_scr[g][rows(st, g), :] = l

    for g, (_, dil) in enumerate(GROUPS):
        n_blk = seq // (dil * Q_BLK)
        step = GROUP_PITCH[g] * Q_BLK
        if n_blk <= UNITS_PER_ITER:
            n_cls = UNITS_PER_ITER // n_blk
            assert dil % n_cls == 0

            def classes(i, carry, g=g, dil=dil, n_blk=n_blk, step=step, n_cls=n_cls):
                unit_batch(g, dil, [([i * n_cls + c + a * step for a in range(n_blk)], None)
                                    for c in range(n_cls)])
                return carry

            lax.fori_loop(0, dil // n_cls, classes, 0)
        else:
            assert n_blk % UNITS_PER_ITER == 0

            def per_class(r, carry, g=g, dil=dil, n_blk=n_blk, step=step):
                unit_batch(g, dil, [([r + a * step for a in range(UNITS_PER_ITER)], None)])

                def later(i, c2):
                    s0 = r + i * (UNITS_PER_ITER * step)
                    unit_batch(g, dil, [([s0 + a * step for a in range(UNITS_PER_ITER)], s0 - step)])
                    return c2

                lax.fori_loop(1, n_blk // UNITS_PER_ITER, later, 0)
                return carry

            lax.fori_loop(0, dil, per_class, 0)

    chunk = 256
    z_scr = h_scr[N_COL_BLOCKS - 1]

    def merge(i, carry):
        t0 = pl.multiple_of(i * chunk, chunk)
        ms = [load_tokens(m_scr[g], g, t0, chunk) for g in range(N_GROUPS)]
        mx = functools.reduce(jnp.maximum, ms)
        es = [jnp.exp(mv - mx) for mv in ms]
        num = functools.reduce(lambda a, b: a + b,
                               [es[g] * load_tokens(o_scr[g], g, t0, chunk) for g in range(N_GROUPS)])
        den = functools.reduce(lambda a, b: a + b,
                               [es[g] * load_tokens(l_scr[g], g, t0, chunk) for g in range(N_GROUPS)])
        z = z_scr[pl.ds(t0, chunk), :]
        y_ref[pl.ds(t0, chunk), :] = ((num / den) * _silu(z)).astype(BF16)
        return carry

    lax.fori_loop(0, seq // chunk, merge, 0)


def _attn_prompt(xb, w_in, li, prevs):
    nbatch, seq, _ = xb.shape
    n_layers = w_in.shape[0]
    n_pair = E_ATTN // LANES
    assert seq % (GROUPS[-1][1] * Q_BLK) == 0 and seq % MM_ROWS == 0 and N_COL_BLOCKS % MM_BLOCKS == 0
    keeps = [min(window, seq) for window, _ in GROUPS]
    scr_rows = [seq // dil * pitch for (_, dil), pitch in zip(GROUPS, GROUP_PITCH)]

    def kv_spec(keep):
        return pl.BlockSpec((None, None, 2, LANES, keep), lambda b, hp: (li, b, 0, hp, 0))

    in_specs = [pl.BlockSpec((None, seq, D_MODEL), lambda b, hp: (b, 0, 0))]
    in_specs += [pl.BlockSpec((None, D_MODEL, LANES), lambda b, hp, blk=blk: (li, 0, blk * n_pair + hp))
                 for blk in range(N_COL_BLOCKS)]
    args = [xb] + [w_in] * N_COL_BLOCKS
    aliases = {}
    if prevs is not None:
        for g in range(N_GROUPS):
            aliases[len(args)] = 1 + g
            in_specs.append(pl.BlockSpec(memory_space=pl.ANY))
            args.append(prevs[g])
    res = pl.pallas_call(
        functools.partial(_attn_prompt_kernel, seq=seq),
        grid=(nbatch, n_pair),
        in_specs=in_specs,
        out_specs=[pl.BlockSpec((None, seq, LANES), lambda b, hp: (b, 0, hp))] + [kv_spec(k) for k in keeps],
        out_shape=[jax.ShapeDtypeStruct((nbatch, seq, E_ATTN), BF16)]
        + [jax.ShapeDtypeStruct((n_layers, nbatch, 2, E_ATTN, k), F32) for k in keeps],
        scratch_shapes=[pltpu.VMEM((scr_rows[blk // 3] if blk < 3 * N_GROUPS else seq, LANES), F32)
                        for blk in range(N_COL_BLOCKS)]
        + [pltpu.VMEM((scr_rows[g], LANES), F32) for _ in range(3) for g in range(N_GROUPS)],
        input_output_aliases=aliases,
        compiler_params=pltpu.CompilerParams(
            dimension_semantics=("arbitrary", "arbitrary"), vmem_limit_bytes=VMEM_LIMIT),
        name="attn_prompt",
    )(*args)
    return res[0], list(res[1:])


def _sample_biases(window, dil, t_new, n_heads):
    qi = np.arange(n_heads * t_new)[:, None] % t_new
    old_off = np.arange(window)[None, :] - window
    new_off = np.arange(LANES)[None, :] - (LANES - t_new)
    res = []
    for off, slot_ok in ((old_off, old_off < 0), (new_off, new_off >= 0)):
        dist = qi - off
        ok = slot_ok & (dist >= 0) & (dist <= window) & (dist % dil == 0)
        res.append(np.where(ok, 0.0, NEG_INF).astype(np.float32))
    return res


def _attn_sample_kernel(*refs, t_new):
    n_in = 10 + 3 * N_GROUPS
    h_refs = refs[:10]
    c_refs = refs[10:10 + N_GROUPS]
    b_refs = refs[10 + N_GROUPS:n_in]
    y_ref = refs[-(N_GROUPS + 1)]
    o_refs = refs[-N_GROUPS:]

    width = SAMPLE_HEADS * HEAD_DIM
    n_rows = SAMPLE_HEADS * t_new
    row = lax.broadcasted_iota(jnp.int32, (n_rows, width), 0)
    col = lax.broadcasted_iota(jnp.int32, (n_rows, width), 1)
    diag = jnp.right_shift(row, t_new.bit_length() - 1) == jnp.right_shift(col, HEAD_DIM.bit_length() - 1)
    new_lanes = lax.broadcasted_iota(jnp.int32, (width, LANES), 1) >= LANES - t_new
    zpad = jnp.zeros((LANES - t_new, width), F32)

    outs, lses = [], []
    for g, (window, _) in enumerate(GROUPS):
        q = h_refs[3 * g][...] * QK_SCALE
        kt_new = jnp.concatenate([zpad, h_refs[3 * g + 1][...]], axis=0).T
        vt_new = jnp.concatenate([zpad, h_refs[3 * g + 2][...]], axis=0).T
        kt = c_refs[g][0]
        vt = c_refs[g][1]
        for idx, (old, new) in enumerate(((kt, kt_new), (vt, vt_new))):
            rolled = pltpu.roll(old, window - t_new, axis=1)
            if window > LANES:
                o_refs[g][idx, :, 0:window - LANES] = rolled[:, 0:window - LANES]
            o_refs[g][idx, :, window - LANES:window] = jnp.where(new_lanes, new, rolled[:, window - LANES:])

        q_rep = jnp.broadcast_to(q[None], (SAMPLE_HEADS, t_new, width)).reshape(n_rows, width)
        q_bd = jnp.where(diag, q_rep, 0.0).astype(BF16)
        s_old = _dot(q_bd, kt.astype(BF16)) + b_refs[2 * g][...]
        s_new = _dot(q_bd, kt_new.astype(BF16)) + b_refs[2 * g + 1][...]
        m = jnp.maximum(jnp.max(s_old, axis=-1, keepdims=True), jnp.max(s_new, axis=-1, keepdims=True))
        p_old = jnp.exp(s_old - m)
        p_new = jnp.exp(s_new - m)
        l = jnp.sum(p_old, axis=-1, keepdims=True) + jnp.sum(p_new, axis=-1, keepdims=True)
        o = (_dot_nt(p_old.astype(BF16), vt.astype(BF16))
             + _dot_nt(p_new.astype(BF16), vt_new.astype(BF16))) * (1.0 / l)
        lse = m + jnp.log(l)
        outs.append(jnp.where(diag, o, 0.0).reshape(SAMPLE_HEADS, t_new, width).sum(axis=0))
        lses.append(jnp.where(diag, lse, 0.0).reshape(SAMPLE_HEADS, t_new, width).sum(axis=0))

    z = h_refs[9][...]
    y_ref[...] = _merge_groups(outs, lses) * _silu(z)


def _attn_sample(h, caches_t, li, prevs):
    nbatch, t_new, _ = h.shape
    n_layers = caches_t[0].shape[0]
    width = SAMPLE_HEADS * HEAD_DIM
    n_chunk = E_ATTN // width
    assert t_new & (t_new - 1) == 0 and t_new <= SUBLANES

    def h_spec(blk):
        return pl.BlockSpec((None, t_new, width), lambda b, c: (b, 0, blk * n_chunk + c))

    def cache_spec(window):
        return pl.BlockSpec((None, None, 2, width, window), lambda b, c: (li, b, 0, c, 0))

    biases = []
    for window, dil in GROUPS:
        biases += [jnp.asarray(a) for a in _sample_biases(window, dil, t_new, SAMPLE_HEADS)]
    full = lambda a: pl.BlockSpec(a.shape, lambda b, c: (0, 0))

    in_specs = [h_spec(blk) for blk in range(10)]
    in_specs += [cache_spec(window) for window, _ in GROUPS]
    in_specs += [full(a) for a in biases]
    args = [h] * 10 + list(caches_t) + biases
    aliases = {}
    if prevs is not None:
        for g in range(N_GROUPS):
            aliases[len(args)] = 1 + g
            in_specs.append(pl.BlockSpec(memory_space=pl.ANY))
            args.append(prevs[g])
    res = pl.pallas_call(
        functools.partial(_attn_sample_kernel, t_new=t_new),
        grid=(nbatch, n_chunk),
        in_specs=in_specs,
        out_specs=[pl.BlockSpec((None, t_new, width), lambda b, c: (b, 0, c))]
        + [cache_spec(window) for window, _ in GROUPS],
        out_shape=[jax.ShapeDtypeStruct((nbatch, t_new, E_ATTN), F32)]
        + [jax.ShapeDtypeStruct((n_layers, nbatch, 2, E_ATTN, window), F32) for window, _ in GROUPS],
        input_output_aliases=aliases,
        compiler_params=pltpu.CompilerParams(
            dimension_semantics=("arbitrary", "arbitrary"), vmem_limit_bytes=VMEM_LIMIT),
        name="attn_sample",
    )(*args)
    return res[0], list(res[1:])


def _cache_time_minor(c):
    n_layers, nbatch, window = c.shape[:3]
    return jnp.transpose(c, (0, 1, 3, 4, 5, 2)).reshape(n_layers, nbatch, 2, E_ATTN, window)


def _cache_time_major(c):
    n_layers, nbatch, window = c.shape[0], c.shape[1], c.shape[-1]
    c = c.reshape(n_layers, nbatch, 2, N_HEADS, HEAD_DIM, window)
    return jnp.transpose(c, (0, 1, 5, 2, 3, 4))


def kernel(x_prompt, x_sample, cache_kv_w128_d1, cache_kv_w512_d4, cache_kv_w2048_d16, state_conv,
           conv_w_in, conv_dw_w, conv_dw_b, conv_ln_g, conv_ln_b, conv_w_out,
           attn_w_in, attn_w_out, norm_g, norm_b):
    caches_t = [_cache_time_minor(c) for c in (cache_kv_w128_d1, cache_kv_w512_d4, cache_kv_w2048_d16)]
    conv_w_in_b = conv_w_in.astype(BF16)
    conv_w_out_b = conv_w_out.astype(BF16)
    attn_w_out_b = attn_w_out.astype(BF16)

    yp, ys = x_prompt, x_sample
    bp, tp, _ = yp.shape
    bs, ts, _ = ys.shape
    kv_p = None
    kv_s = None
    conv_p, conv_s = [], []
    for i in range(DEPTH):
        li = i // 2
        if i % 2 == 0:
            prm = (conv_w_in_b[li], conv_dw_w[li], conv_dw_b[li], conv_ln_g[li], conv_ln_b[li],
                   conv_w_out_b[li], norm_g[i], norm_b[i])
            zero_buf = jnp.zeros((bp, CONV_WIDTH - 1, E_CONV), F32)
            yp, yp_b, stp = _conv_layer(yp, zero_buf, *prm, emit_bf16=True)
            ys, _, sts = _conv_layer(ys, state_conv[li], *prm, emit_bf16=False)
            conv_p.append(stp)
            conv_s.append(sts)
        else:
            hs = _matmul(ys.reshape(bs * ts, D_MODEL), attn_w_in, li).reshape(bs, ts, -1)
            op, kv_p = _attn_prompt(yp_b, attn_w_in, li, kv_p)
            os_, kv_s = _attn_sample(hs, caches_t, li, kv_s)
            yp = _proj_ln(op.reshape(bp * tp, E_ATTN), yp.reshape(bp * tp, D_MODEL),
                          attn_w_out_b[li], norm_g[i], norm_b[i]).reshape(bp, tp, D_MODEL)
            ys = _proj_ln(os_.reshape(bs * ts, E_ATTN), ys.reshape(bs * ts, D_MODEL),
                          attn_w_out_b[li], norm_g[i], norm_b[i]).reshape(bs, ts, D_MODEL)
    return (yp, ys,
            _cache_time_major(kv_p[0]), _cache_time_major(kv_p[1]), _cache_time_major(kv_p[2]),
            jnp.stack(conv_p),
            _cache_time_major(kv_s[0]), _cache_time_major(kv_s[1]), _cache_time_major(kv_s[2]),
            jnp.stack(conv_s))
```
